```python
import jax, jax.numpy as jnp
from jax import lax
import numpy as np

D_MODEL = 1024
BATCH = 2
SEQ = 8192
DEPTH = 1
DEC_BATCH = 128
DEC_SEQ = 4
PAST_LEN = 8192
PAGE_SIZE = 128

D_PLE = 256
RET_HEADS = 4
RET_DK = 128
RET_DV = 128
SB_HEADS = 8
SB_DH = 64
RET_W = RET_HEADS * RET_DV
SB_W = SB_HEADS * SB_DH
MIX_W = RET_W + SB_W
PROJ_SIZES = (RET_HEADS * RET_DK, RET_HEADS * RET_DK, RET_W, RET_W, SB_W, SB_W, SB_W, SB_W)
D_IN = sum(PROJ_SIZES)
CHUNK = 128
QBLOCK = 128
SB_BIAS_INIT = -8.0
ROPE_BASE = 10000.0
EPS = 1e-6

kernel_name = "hybrid_retention_stickbreaking_decode_step"


def rms_norm(x, g):
    xf = x.astype(jnp.float32)
    y = xf * lax.rsqrt(jnp.mean(xf * xf, axis=-1, keepdims=True) + EPS)
    return (y * g.astype(jnp.float32)).astype(x.dtype)


def rotary(x, pos):
    half = x.shape[-1] // 2
    inv = ROPE_BASE ** (-jnp.arange(half, dtype=jnp.float32) / half)
    ang = pos[:, None] * inv[None, :]
    cos = jnp.cos(ang)[:, None, :]
    sin = jnp.sin(ang)[:, None, :]
    x1, x2 = x[..., :half], x[..., half:]
    return jnp.concatenate([x1 * cos - x2 * sin, x1 * sin + x2 * cos], axis=-1)


def split_proj(u, w_in):
    proj = jnp.einsum('btd,de->bte', u, w_in)
    offs = tuple(int(o) for o in np.cumsum(PROJ_SIZES)[:-1])
    rq, rk, rv, rg, sq, sk, sv, sg = jnp.split(proj, offs, axis=-1)
    b, t = u.shape[0], u.shape[1]
    rq = rq.reshape(b, t, RET_HEADS, RET_DK)
    rk = rk.reshape(b, t, RET_HEADS, RET_DK)
    rv = rv.reshape(b, t, RET_HEADS, RET_DV)
    sq = sq.reshape(b, t, SB_HEADS, SB_DH)
    sk = sk.reshape(b, t, SB_HEADS, SB_DH)
    sv = sv.reshape(b, t, SB_HEADS, SB_DH)
    return rq, rk, rv, rg, sq, sk, sv, sg


def retention_chunk(q, k, v, s0, log_g):
    c = q.shape[1]
    n = jnp.arange(c, dtype=jnp.float32)
    diff = n[:, None] - n[None, :]
    causal = diff >= 0
    decay = jnp.where(causal[None], jnp.exp(jnp.where(causal, diff, 0.0)[None] * log_g[:, None, None]), 0.0)
    inner = jnp.einsum('bnhd,bmhd->bhnm', q, k) * decay[None]
    o = jnp.einsum('bhnm,bmhe->bnhe', inner, v)
    q_dec = jnp.exp((n + 1.0)[:, None] * log_g[None, :])
    o = o + jnp.einsum('bnhd,bhde->bnhe', q * q_dec[None, :, :, None], s0)
    k_dec = jnp.exp((c - 1.0 - n)[:, None] * log_g[None, :])
    s1 = jnp.exp(c * log_g)[None, :, None, None] * s0 + jnp.einsum('bmhd,bmhe->bhde', k * k_dec[None, :, :, None], v)
    return o, s1


def retention(q, k, v, s0, chunk):
    b, t, h, dk = q.shape
    dv = v.shape[-1]
    nc = t // chunk
    log_g = jnp.log(1.0 - 2.0 ** (-5.0 - jnp.arange(RET_HEADS, dtype=jnp.float32)))
    qc = q.reshape(b, nc, chunk, h, dk).swapaxes(0, 1)
    kc = k.reshape(b, nc, chunk, h, dk).swapaxes(0, 1)
    vc = v.reshape(b, nc, chunk, h, dv).swapaxes(0, 1)

    def step(s, inp):
        o, s_new = retention_chunk(inp[0], inp[1], inp[2], s, log_g)
        return s_new, o

    s_fin, o = lax.scan(step, s0, (qc, kc, vc))
    return o.swapaxes(0, 1).reshape(b, t, h, dv), s_fin


def ret_branch(rq, rk, rv, pos, s0, chunk):
    q = rotary(rq.astype(jnp.float32), pos)
    k = rotary(rk.astype(jnp.float32), pos) * (RET_DK ** -0.5)
    return retention(q, k, rv.astype(jnp.float32), s0.astype(jnp.float32), chunk)


def sb_attend(q, k, v, bias, q_pos, k_pos):
    z = jnp.einsum('...qhd,...khd->...hqk', q.astype(jnp.float32), k.astype(jnp.float32)) * (SB_DH ** -0.5)
    z = z + bias.astype(jnp.float32)[:, None, None]
    mask = k_pos[None, :] < q_pos[:, None]
    lnb = jnp.where(mask, jax.nn.log_sigmoid(-z), 0.0)
    rest = lax.cumsum(lnb, axis=z.ndim - 1, reverse=True) - lnb
    a = jnp.where(mask, jnp.exp(jax.nn.log_sigmoid(z) + rest), 0.0)
    return jnp.einsum('...hqk,...khd->...qhd', a, v.astype(jnp.float32))


def sb_prompt(q, k, v, bias):
    b, t = q.shape[0], q.shape[1]
    nb = t // QBLOCK
    qb = q.reshape(b, nb, QBLOCK, SB_HEADS, SB_DH).swapaxes(0, 1)
    starts = jnp.arange(nb, dtype=jnp.int32) * QBLOCK
    k_pos = jnp.arange(t, dtype=jnp.int32)
    out = lax.map(lambda a: sb_attend(a[0], k, v, bias, a[1] + jnp.arange(QBLOCK, dtype=jnp.int32), k_pos), (qb, starts))
    return out.swapaxes(0, 1).reshape(b, t, SB_HEADS, SB_DH)


def sb_sample(q, k, v, bias, cache_k, cache_v, layer, page_table):
    n_dec = q.shape[1]
    past = page_table.shape[1] * cache_k.shape[2]
    q_pos = past + jnp.arange(n_dec, dtype=jnp.int32)
    k_pos = jnp.arange(past + n_dec, dtype=jnp.int32)

    def one(a):
        pt, qs, ks, vs = a
        kp = cache_k[layer, pt].reshape(past, SB_HEADS, SB_DH)
        vp = cache_v[layer, pt].reshape(past, SB_HEADS, SB_DH)
        kk = jnp.concatenate([kp.astype(jnp.float32), ks.astype(jnp.float32)], axis=0)
        vv = jnp.concatenate([vp.astype(jnp.float32), vs.astype(jnp.float32)], axis=0)
        return sb_attend(qs, kk, vv, bias, q_pos, k_pos)

    return lax.map(one, (page_table, q, k, v))


def head_group_norm(o, g):
    mu = jnp.mean(o, axis=-1, keepdims=True)
    var = jnp.mean(jnp.square(o - mu), axis=-1, keepdims=True)
    y = (o - mu) * lax.rsqrt(var + EPS)
    return y.reshape(o.shape[0], o.shape[1], -1) * g.astype(jnp.float32)


def merge(o_ret, g_gn, rg, o_sb, sg, w_out, dtype):
    yr = head_group_norm(o_ret, g_gn) * jax.nn.silu(rg.astype(jnp.float32))
    ys = o_sb.reshape(o_sb.shape[0], o_sb.shape[1], SB_W) * jax.nn.silu(sg.astype(jnp.float32))
    y = jnp.concatenate([yr, ys], axis=-1).astype(dtype)
    return jnp.einsum('bte,ed->btd', y, w_out)


def per_layer_embed(h, p, w_ple, g_ple, w_gate):
    e = rms_norm(jnp.einsum('btp,pd->btd', p, w_ple), g_ple)
    gate = jax.nn.sigmoid(jnp.einsum('btd,de->bte', h, w_gate).astype(jnp.float32))
    return h + (e.astype(jnp.float32) * gate).astype(h.dtype)


def setup_inputs(seed: int = 0) -> dict:
    key = jax.random.key(seed)
    ks = jax.random.split(key, 20)
    f32 = jnp.float32
    n_pages = PAST_LEN // PAGE_SIZE
    n_used = DEC_BATCH * n_pages
    n_pool = n_used + n_used // 4
    page_table = jax.random.permutation(ks[0], n_pool)[:n_used].reshape(DEC_BATCH, n_pages).astype(jnp.int32)
    return {
        "x_prompt": jax.random.normal(ks[1], (BATCH, SEQ, D_MODEL), f32),
        "x_sample": jax.random.normal(ks[2], (DEC_BATCH, DEC_SEQ, D_MODEL), f32),
        "cache_sb_k": jax.random.normal(ks[3], (DEPTH, n_pool, PAGE_SIZE, SB_HEADS, SB_DH), f32),
        "cache_sb_v": jax.random.normal(ks[4], (DEPTH, n_pool, PAGE_SIZE, SB_HEADS, SB_DH), f32),
        "state_ret": jax.random.normal(ks[5], (DEPTH, DEC_BATCH, RET_HEADS, RET_DK, RET_DV), f32),
        "page_table": page_table,
        "p_prompt": jax.random.normal(ks[6], (DEPTH, BATCH, SEQ, D_PLE), f32),
        "p_sample": jax.random.normal(ks[7], (DEPTH, DEC_BATCH, DEC_SEQ, D_PLE), f32),
        "g_norm": 1.0 + 0.01 * jax.random.normal(ks[8], (DEPTH, D_MODEL), f32),
        "w_in": jax.random.normal(ks[9], (DEPTH, D_MODEL, D_IN), f32) * D_MODEL ** -0.5,
        "sb_bias": SB_BIAS_INIT + 0.1 * jax.random.normal(ks[16], (DEPTH, SB_HEADS), f32),
        "g_ret_gn": 1.0 + 0.01 * jax.random.normal(ks[10], (DEPTH, RET_W), f32),
        "w_out": jax.random.normal(ks[11], (DEPTH, MIX_W, D_MODEL), f32) * MIX_W ** -0.5,
        "w_ple": jax.random.normal(ks[12], (DEPTH, D_PLE, D_MODEL), f32) * D_PLE ** -0.5,
        "g_ple": 1.0 + 0.01 * jax.random.normal(ks[13], (DEPTH, D_MODEL), f32),
        "w_ple_gate": jax.random.normal(ks[14], (DEPTH, D_MODEL, D_MODEL), f32) * D_MODEL ** -0.5,
        "g_final": 1.0 + 0.01 * jax.random.normal(ks[15], (D_MODEL,), f32),
    }


def reference(x_prompt, x_sample, cache_sb_k, cache_sb_v, state_ret, page_table, p_prompt, p_sample,
              g_norm, w_in, sb_bias, g_ret_gn, w_out, w_ple, g_ple, w_ple_gate, g_final):
    b, t = x_prompt.shape[0], x_prompt.shape[1]
    db, dt = x_sample.shape[0], x_sample.shape[1]
    past = page_table.shape[1] * cache_sb_k.shape[2]
    pos_p = jnp.arange(t, dtype=jnp.float32)
    pos_s = past + jnp.arange(dt, dtype=jnp.float32)
    hp, hs = x_prompt, x_sample
    kp_l, vp_l, ks_l, vs_l, rp_l, rs_l = [], [], [], [], [], []
    for i in range(DEPTH):
        up = rms_norm(hp, g_norm[i])
        rq, rk, rv, rg, sq, sk, sv, sg = split_proj(up, w_in[i])
        s0 = jnp.zeros((b, RET_HEADS, RET_DK, RET_DV), jnp.float32)
        o_r, s_p = ret_branch(rq, rk, rv, pos_p, s0, CHUNK)
        o_s = sb_prompt(sq, sk, sv, sb_bias[i])
        hp = hp + merge(o_r, g_ret_gn[i], rg, o_s, sg, w_out[i], hp.dtype)
        hp = per_layer_embed(hp, p_prompt[i], w_ple[i], g_ple[i], w_ple_gate[i])
        kp_l.append(sk)
        vp_l.append(sv)
        rp_l.append(s_p)
        us = rms_norm(hs, g_norm[i])
        rq, rk, rv, rg, sq, sk, sv, sg = split_proj(us, w_in[i])
        o_r, s_s = ret_branch(rq, rk, rv, pos_s, state_ret[i], dt)
        o_s = sb_sample(sq, sk, sv, sb_bias[i], cache_sb_k, cache_sb_v, i, page_table)
        hs = hs + merge(o_r, g_ret_gn[i], rg, o_s, sg, w_out[i], hs.dtype)
        hs = per_layer_embed(hs, p_sample[i], w_ple[i], g_ple[i], w_ple_gate[i])
        ks_l.append(sk)
        vs_l.append(sv)
        rs_l.append(s_s)
    y_prompt = rms_norm(hp, g_final)
    y_sample = rms_norm(hs, g_final)
    new_k_prompt = jnp.stack(kp_l)
    new_v_prompt = jnp.stack(vp_l)
    new_k_sample = jnp.stack(ks_l)
    new_v_sample = jnp.stack(vs_l)
    new_ret_prompt = jnp.stack(rp_l)
    new_ret_sample = jnp.stack(rs_l)
    return (y_prompt, y_sample, new_k_prompt, new_v_prompt, new_k_sample, new_v_sample, new_ret_prompt, new_ret_sample)
```

```python
import functools
import math

import numpy as np
import jax
import jax.numpy as jnp
from jax import lax
from jax.experimental import pallas as pl
from jax.experimental.pallas import tpu as pltpu

F32 = jnp.float32
BF16 = jnp.bfloat16

RET_HEADS = 4
RET_D = 128
SB_HEADS = 8
SB_DH = 64
SEG_W = 512
CHUNK = 128
ROPE_BASE = 10000.0
EPS = 1e-6
LOG2E = math.log2(math.e)
LANES = 128
VMEM_LIMIT = 48 * 1024 * 1024

LOG_G = [float(np.log(np.float32(1.0) - np.float32(2.0) ** np.float32(-5.0 - h))) for h in range(RET_HEADS)]


def _nt_dot(a, b):
    return lax.dot_general(a, b, (((1,), (1,)), ((), ())), preferred_element_type=F32)


def _tn_dot(a, b):
    return lax.dot_general(a, b, (((0,), (0,)), ((), ())), preferred_element_type=F32)


def _dot(a, b):
    return jnp.dot(a, b, preferred_element_type=F32)


def _silu(x):
    return x / (1.0 + jnp.exp(-x))


def _rope_kernel(inv_ref, cos_ref, sin_ref, *, base, period, rows):
    r = lax.broadcasted_iota(jnp.int32, (rows, LANES), 0) + pl.program_id(0) * rows
    lane = lax.broadcasted_iota(jnp.int32, (rows, LANES), 1)
    pos = (base + r % period).astype(F32)
    ang = pos * inv_ref[...]
    cos_ref[...] = jnp.cos(ang)
    sin_ref[...] = jnp.where(lane < LANES // 2, -1.0, 1.0) * jnp.sin(ang)


def _rope_table(inv128, n_rows, base, period):
    rows = min(n_rows, 512)
    return pl.pallas_call(
        functools.partial(_rope_kernel, base=base, period=period, rows=rows),
        grid=(n_rows // rows,),
        in_specs=[pl.BlockSpec((1, LANES), lambda i: (0, 0))],
        out_specs=[pl.BlockSpec((rows, LANES), lambda i: (i, 0))] * 2,
        out_shape=[jax.ShapeDtypeStruct((n_rows, LANES), F32)] * 2,
        name="rope_table",
    )(inv128)


def _proj_kernel(x_ref, g_ref, w_ref, cos_ref, sin_ref,
                 rq_ref, rk_ref, rv_ref, rg_ref, sqb_ref, sk_ref, sv_ref, sg_ref, skb_ref, svb_ref):
    x = x_ref[...]
    ms = jnp.mean(x * x, axis=-1, keepdims=True)
    xn = ((x * lax.rsqrt(ms + EPS)) * g_ref[...]).astype(BF16)
    cos = cos_ref[...]
    sin = sin_ref[...]

    def seg(j):
        return _dot(xn, w_ref[:, j * SEG_W:(j + 1) * SEG_W])

    def rotary_store(p, out_ref, scale):
        for h in range(RET_HEADS):
            ph = p[:, h * RET_D:(h + 1) * RET_D]
            rot = ph * cos + pltpu.roll(ph, RET_D // 2, axis=1) * sin
            out_ref[:, h * RET_D:(h + 1) * RET_D] = rot if scale is None else rot * scale

    rotary_store(seg(0), rq_ref, None)
    rotary_store(seg(1), rk_ref, RET_D ** -0.5)
    rv_ref[...] = seg(2)
    rg_ref[...] = seg(3)
    sqb_ref[...] = (seg(4) * (LOG2E * SB_DH ** -0.5)).astype(BF16)
    sk = seg(5)
    sk_ref[...] = sk
    skb_ref[...] = sk.astype(BF16)
    sv = seg(6)
    sv_ref[...] = sv
    svb_ref[...] = sv.astype(BF16)
    sg_ref[...] = seg(7)


def _project(x2d, g_norm, w_bf, cos_t, sin_t, table_period_blocks, tm):
    n, d = x2d.shape
    d_in = w_bf.shape[1]
    row = lambda i: (i, 0)
    f32_out = jax.ShapeDtypeStruct((n, SEG_W), F32)
    bf_out = jax.ShapeDtypeStruct((n, SEG_W), BF16)
    seg_spec = pl.BlockSpec((tm, SEG_W), row)
    tab_spec = pl.BlockSpec((tm, LANES), lambda i: (i % table_period_blocks, 0))
    return pl.pallas_call(
        _proj_kernel,
        grid=(n // tm,),
        in_specs=[pl.BlockSpec((tm, d), row),
                  pl.BlockSpec((1, d), lambda i: (0, 0)),
                  pl.BlockSpec((d, d_in), lambda i: (0, 0)),
                  tab_spec, tab_spec],
        out_specs=[seg_spec] * 10,
        out_shape=[f32_out, f32_out, f32_out, f32_out, bf_out, f32_out, f32_out, f32_out, bf_out, bf_out],
        compiler_params=pltpu.CompilerParams(dimension_semantics=("parallel",), vmem_limit_bytes=VMEM_LIMIT),
        name="rmsnorm_in_proj",
    )(x2d, g_norm, w_bf, cos_t, sin_t)


def _group_norm_gate(o, gn, rg):
    mu = jnp.mean(o, axis=-1, keepdims=True)
    d = o - mu
    var = jnp.mean(d * d, axis=-1, keepdims=True)
    return (d * lax.rsqrt(var + EPS)) * gn * _silu(rg)


def _ret_prompt_kernel(q_ref, k_ref, v_ref, rg_ref, gn_ref, y_ref, sfin_ref,
                       state_ref, decay_ref, qdec_ref, kdec_ref, *, batch, chunk):
    c = pl.program_id(0)

    @pl.when(c == 0)
    def _init():
        state_ref[...] = jnp.zeros_like(state_ref)
        n = lax.broadcasted_iota(jnp.int32, (chunk, chunk), 0).astype(F32)
        m = lax.broadcasted_iota(jnp.int32, (chunk, chunk), 1).astype(F32)
        diff = n - m
        causal = diff >= 0
        for h in range(RET_HEADS):
            decay_ref[h] = jnp.where(causal, jnp.exp(jnp.where(causal, diff, 0.0) * LOG_G[h]), 0.0)
            qdec_ref[h] = jnp.exp((n + 1.0) * LOG_G[h])
            kdec_ref[h] = jnp.exp((chunk - 1.0 - n) * LOG_G[h])

    for b in range(batch):
        for h in range(RET_HEADS):
            cols = slice(h * RET_D, (h + 1) * RET_D)
            q = q_ref[b, :, cols]
            k = k_ref[b, :, cols]
            vb = v_ref[b, :, cols].astype(BF16)
            s = state_ref[b * RET_HEADS + h]
            inner = _nt_dot(q.astype(BF16), k.astype(BF16)) * decay_ref[h]
            o = _dot(inner.astype(BF16), vb) + _dot((q * qdec_ref[h]).astype(BF16), s.astype(BF16))
            state_ref[b * RET_HEADS + h] = (math.exp(chunk * LOG_G[h]) * s
                                            + _tn_dot((k * kdec_ref[h]).astype(BF16), vb))
            y_ref[b, :, cols] = _group_norm_gate(o, gn_ref[:, cols], rg_ref[b, :, cols])

    @pl.when(c == pl.num_programs(0) - 1)
    def _fin():
        sfin_ref[...] = state_ref[...]


def _retention_prompt(rq, rk, rv, rg, gn, batch, seq):
    nc = seq // CHUNK
    blk = pl.BlockSpec((batch, CHUNK, SEG_W), lambda c: (0, c, 0))
    sq_mat = pltpu.VMEM((RET_HEADS, CHUNK, CHUNK), F32)
    return pl.pallas_call(
        functools.partial(_ret_prompt_kernel, batch=batch, chunk=CHUNK),
        grid=(nc,),
        in_specs=[blk, blk, blk, blk, pl.BlockSpec((1, SEG_W), lambda c: (0, 0))],
        out_specs=[blk, pl.BlockSpec((batch * RET_HEADS, RET_D, RET_D), lambda c: (0, 0, 0))],
        out_shape=[jax.ShapeDtypeStruct((batch, seq, SEG_W), F32),
                   jax.ShapeDtypeStruct((batch * RET_HEADS, RET_D, RET_D), F32)],
        scratch_shapes=[pltpu.VMEM((batch * RET_HEADS, RET_D, RET_D), F32), sq_mat, sq_mat, sq_mat],
        compiler_params=pltpu.CompilerParams(dimension_semantics=("arbitrary",), vmem_limit_bytes=VMEM_LIMIT),
        name="retention_prompt",
    )(rq.reshape(batch, seq, SEG_W), rk.reshape(batch, seq, SEG_W), rv.reshape(batch, seq, SEG_W),
      rg.reshape(batch, seq, SEG_W), gn)


def _ret_sample_kernel(q_ref, k_ref, v_ref, rg_ref, gn_ref, s0_ref, y_ref, s1_ref, *, bb, dt):
    grp = 16 // dt
    rows = grp * dt
    ri = lax.broadcasted_iota(jnp.int32, (rows, rows), 0)
    ci = lax.broadcasted_iota(jnp.int32, (rows, rows), 1)
    diff = (ri % dt - ci % dt).astype(F32)
    keep = (ri // dt == ci // dt) & (diff >= 0)
    rn = (lax.broadcasted_iota(jnp.int32, (rows, RET_D), 0) % dt).astype(F32)
    rb = lax.broadcasted_iota(jnp.int32, (rows, RET_D), 0) // dt
    for g in range(bb // grp):
        rsl = slice(g * rows, (g + 1) * rows)
        for h in range(RET_HEADS):
            cols = slice(h * RET_D, (h + 1) * RET_D)
            lg = LOG_G[h]
            decay = jnp.where(keep, jnp.exp(jnp.where(keep, diff, 0.0) * lg), 0.0)
            q = q_ref[rsl, cols]
            k = k_ref[rsl, cols]
            vb = v_ref[rsl, cols].astype(BF16)
            inner = _nt_dot(q.astype(BF16), k.astype(BF16)) * decay
            o = _dot(inner.astype(BF16), vb)
            qd = (q * jnp.exp((rn + 1.0) * lg)).astype(BF16)
            kd = k * jnp.exp((dt - 1.0 - rn) * lg)
            for j in range(grp):
                b = g * grp + j
                s0 = s0_ref[b, h]
                o = o + jnp.where(rb == j, _dot(qd, s0.astype(BF16)), 0.0)
                kdj = jnp.where(rb == j, kd, 0.0).astype(BF16)
                s1_ref[b, h] = math.exp(dt * lg) * s0 + _tn_dot(kdj, vb)
            y_ref[rsl, cols] = _group_norm_gate(o, gn_ref[:, cols], rg_ref[rsl, cols])


def _retention_sample(rq, rk, rv, rg, gn, state, dt):
    n = rq.shape[0]
    db = n // dt
    bb = 8
    blk = pl.BlockSpec((bb * dt, SEG_W), lambda i: (i, 0))
    st = pl.BlockSpec((bb, RET_HEADS, RET_D, RET_D), lambda i: (i, 0, 0, 0))
    return pl.pallas_call(
        functools.partial(_ret_sample_kernel, bb=bb, dt=dt),
        grid=(db // bb,),
        in_specs=[blk, blk, blk, blk, pl.BlockSpec((1, SEG_W), lambda i: (0, 0)), st],
        out_specs=[blk, st],
        out_shape=[jax.ShapeDtypeStruct((n, SEG_W), F32), jax.ShapeDtypeStruct(state.shape, F32)],
        compiler_params=pltpu.CompilerParams(dimension_semantics=("parallel",), vmem_limit_bytes=VMEM_LIMIT),
        name="retention_sample",
    )(rq, rk, rv, rg, gn, state)


def _tri(n):
    r = lax.broadcasted_iota(jnp.int32, (n, n), 0)
    c = lax.broadcasted_iota(jnp.int32, (n, n), 1)
    return jnp.where(r > c, 1.0, 0.0).astype(BF16)


def _sb_block(z, carry, tri, mask):
    e = jnp.exp2(-jnp.abs(z))
    lnb = -(jnp.maximum(z, 0.0) + jnp.log2(1.0 + e))
    if mask is not None:
        lnb = jnp.where(mask, lnb, 0.0)
    hi = lnb.astype(BF16)
    lo = (lnb - hi.astype(F32)).astype(BF16)
    cum = _dot(hi, tri) + _dot(lo, tri)
    a = jnp.exp2((z + lnb) + (cum + carry))
    if mask is not None:
        a = jnp.where(mask, a, 0.0)
    return a, carry + (cum + lnb)[:, 0:1]


def _sb_prompt_kernel(bias_ref, q_ref, k_ref, v_ref, sg_ref, o_ref, qz_ref, acc_ref, carry_ref, *, tq, tk):
    hp = pl.program_id(1)
    qi = pl.program_id(2)
    nd = tq // tk
    lane = lax.broadcasted_iota(jnp.int32, (tq, LANES), 1)
    q = q_ref[...]
    qz_ref[0] = jnp.where(lane < SB_DH, q, jnp.zeros_like(q))
    qz_ref[1] = jnp.where(lane >= SB_DH, q, jnp.zeros_like(q))
    acc_ref[...] = jnp.zeros_like(acc_ref)
    carry_ref[...] = jnp.zeros_like(carry_ref)
    tri = _tri(tk)
    bias = [bias_ref[2 * hp] * LOG2E, bias_ref[2 * hp + 1] * LOG2E]
    q0 = qi * tq

    def block(k0, mask):
        kb = k_ref[pl.ds(k0, tk), :]
        vb = v_ref[pl.ds(k0, tk), :]
        for h in range(2):
            z = _nt_dot(qz_ref[h], kb) + bias[h]
            a, carry = _sb_block(z, carry_ref[h], tri, mask)
            carry_ref[h] = carry
            acc_ref[h] += _dot(a.astype(BF16), vb)

    rr = lax.broadcasted_iota(jnp.int32, (tq, tk), 0)
    cc = lax.broadcasted_iota(jnp.int32, (tq, tk), 1)
    for d in reversed(range(nd)):
        block(pl.multiple_of(q0 + d * tk, tk), (cc + d * tk) < rr)

    def body(i, _):
        block(pl.multiple_of(q0 - (i + 1) * tk, tk), None)
        return 0

    lax.fori_loop(0, qi * nd, body, 0)
    o = jnp.where(lane < SB_DH, acc_ref[0], acc_ref[1])
    o_ref[...] = o * _silu(sg_ref[...])


def _sb_prompt(sb_bias, sqb, skb, svb, sg, batch, seq, tq, tk):
    nq = seq // tq
    qblk = pl.BlockSpec((tq, LANES), lambda b, hp, i: (b * nq + i, hp))
    kvblk = pl.BlockSpec((seq, LANES), lambda b, hp, i: (b, hp))
    return pl.pallas_call(
        functools.partial(_sb_prompt_kernel, tq=tq, tk=tk),
        grid=(batch, SB_HEADS // 2, nq),
        in_specs=[pl.BlockSpec(memory_space=pltpu.SMEM), qblk, kvblk, kvblk, qblk],
        out_specs=qblk,
        out_shape=jax.ShapeDtypeStruct((batch * seq, SEG_W), F32),
        scratch_shapes=[pltpu.VMEM((2, tq, LANES), BF16), pltpu.VMEM((2, tq, LANES), F32),
                        pltpu.VMEM((2, tq, 1), F32)],
        compiler_params=pltpu.CompilerParams(dimension_semantics=("parallel", "parallel", "arbitrary"),
                                             vmem_limit_bytes=VMEM_LIMIT),
        name="sb_prompt",
    )(sb_bias, sqb, skb, svb, sg)


def _sb_sample_kernel(pt_ref, bias_ref, q_ref, kn_ref, vn_ref, sg_ref, *rest, pp, page, dt):
    k_refs = rest[:pp]
    v_refs = rest[pp:2 * pp]
    o_ref, qbd_ref, acc_ref, carry_ref = rest[2 * pp:]
    j = pl.program_id(1)
    rows = dt * SB_HEADS
    tri = _tri(page)
    bias = bias_ref[...]

    def block(kt, vt, mask):
        z = _dot(qbd_ref[...], kt) + bias
        a, carry = _sb_block(z, carry_ref[...], tri, mask)
        carry_ref[...] = carry
        acc_ref[...] += _nt_dot(a.astype(BF16), vt)

    @pl.when(j == 0)
    def _start():
        r = lax.broadcasted_iota(jnp.int32, (rows, SEG_W), 0)
        c = lax.broadcasted_iota(jnp.int32, (rows, SEG_W), 1)
        q = q_ref[0]
        qbd_ref[...] = jnp.where(c // SB_DH == r % SB_HEADS, q, jnp.zeros_like(q))
        acc_ref[...] = jnp.zeros_like(acc_ref)
        carry_ref[...] = jnp.zeros_like(carry_ref)
        rq = lax.broadcasted_iota(jnp.int32, (rows, page), 0) // SB_HEADS
        cm = lax.broadcasted_iota(jnp.int32, (rows, page), 1)
        block(kn_ref[0], vn_ref[0], cm < rq)

    for i in reversed(range(pp)):
        block(k_refs[i][0].astype(BF16), v_refs[i][0].astype(BF16), None)

    @pl.when(j == pl.num_programs(1) - 1)
    def _fin():
        r = lax.broadcasted_iota(jnp.int32, (rows, SEG_W), 0)
        c = lax.broadcasted_iota(jnp.int32, (rows, SEG_W), 1)
        own = jnp.where(c // SB_DH == r % SB_HEADS, acc_ref[...], 0.0)
        o = jnp.sum(own.reshape(dt, SB_HEADS, SEG_W), axis=1)
        o_ref[0] = o * _silu(sg_ref[0])


def _sb_sample(page_table, bias_col, q_rep, k_new, v_new, sg, cache_k, cache_v, pp):
    db, n_pages = page_table.shape
    page = cache_k.shape[2]
    dt = sg.shape[1]
    rows = dt * SB_HEADS
    nstep = n_pages // pp
    per_b = lambda blk: pl.BlockSpec((1,) + blk, lambda b, j, pt: (b, 0, 0))

    def page_spec(i):
        return pl.BlockSpec((1, SEG_W, page), lambda b, j, pt: (pt[b, (nstep - 1 - j) * pp + i], 0, 0))

    grid_spec = pltpu.PrefetchScalarGridSpec(
        num_scalar_prefetch=1,
        grid=(db, nstep),
        in_specs=[pl.BlockSpec((rows, 1), lambda b, j, pt: (0, 0)),
                  per_b((rows, SEG_W)), per_b((SEG_W, page)), per_b((SEG_W, page)), per_b((dt, SEG_W))]
                 + [page_spec(i) for i in range(pp)] * 2,
        out_specs=per_b((dt, SEG_W)),
        scratch_shapes=[pltpu.VMEM((rows, SEG_W), BF16), pltpu.VMEM((rows, SEG_W), F32),
                        pltpu.VMEM((rows, 1), F32)],
    )
    return pl.pallas_call(
        functools.partial(_sb_sample_kernel, pp=pp, page=page, dt=dt),
        grid_spec=grid_spec,
        out_shape=jax.ShapeDtypeStruct((db, dt, SEG_W), F32),
        compiler_params=pltpu.CompilerParams(dimension_semantics=("parallel", "arbitrary"),
                                             vmem_limit_bytes=VMEM_LIMIT),
        name="sb_sample",
    )(page_table, bias_col, q_rep, k_new, v_new, sg, *([cache_k] * pp), *([cache_v] * pp))


def _out_kernel(yr_ref, ys_ref, h_ref, p_ref, wo_ref, wp_ref, wg_ref, gp_ref, gf_ref, o_ref):
    y = _dot(yr_ref[...].astype(BF16), wo_ref[:SEG_W, :]) + _dot(ys_ref[...].astype(BF16), wo_ref[SEG_W:, :])
    h1 = h_ref[...] + y
    e = _dot(p_ref[...].astype(BF16), wp_ref[...])
    e = (e * lax.rsqrt(jnp.mean(e * e, axis=-1, keepdims=True) + EPS)) * gp_ref[...]
    gate = 1.0 / (1.0 + jnp.exp(-_dot(h1.astype(BF16), wg_ref[...])))
    h2 = h1 + e * gate
    o_ref[...] = (h2 * lax.rsqrt(jnp.mean(h2 * h2, axis=-1, keepdims=True) + EPS)) * gf_ref[...]


def _out_proj(yr, ys, h, p, wo_bf, wp_bf, wg_bf, g_ple, g_final, tm):
    n, d = h.shape
    d_ple = p.shape[1]
    row = lambda i: (i, 0)
    const = lambda i: (0, 0)
    return pl.pallas_call(
        _out_kernel,
        grid=(n // tm,),
        in_specs=[pl.BlockSpec((tm, SEG_W), row), pl.BlockSpec((tm, SEG_W), row), pl.BlockSpec((tm, d), row),
                  pl.BlockSpec((tm, d_ple), row), pl.BlockSpec((2 * SEG_W, d), const),
                  pl.BlockSpec((d_ple, d), const), pl.BlockSpec((d, d), const),
                  pl.BlockSpec((1, d), const), pl.BlockSpec((1, d), const)],
        out_specs=pl.BlockSpec((tm, d), row),
        out_shape=jax.ShapeDtypeStruct((n, d), F32),
        compiler_params=pltpu.CompilerParams(dimension_semantics=("parallel",), vmem_limit_bytes=VMEM_LIMIT),
        name="out_proj_ple_norm",
    )(yr, ys, h, p, wo_bf, wp_bf, wg_bf, g_ple, g_final)


def kernel(x_prompt, x_sample, cache_sb_k, cache_sb_v, state_ret, page_table, p_prompt, p_sample,
           g_norm, w_in, sb_bias, g_ret_gn, w_out, w_ple, g_ple, w_ple_gate, g_final):
    b, t, d = x_prompt.shape
    db, dt, _ = x_sample.shape
    depth = w_in.shape[0]
    assert depth == 1, "single-layer trunk"
    n_pool, page = cache_sb_k.shape[1], cache_sb_k.shape[2]
    past = page_table.shape[1] * page
    tm_p = min(256, t)
    tm_s = min(256, db * dt)

    half = RET_D // 2
    inv = ROPE_BASE ** (-jnp.arange(half, dtype=F32) / half)
    inv128 = jnp.concatenate([inv, inv])[None, :]
    cos_p, sin_p = _rope_table(inv128, t, 0, t)
    cos_s, sin_s = _rope_table(inv128, tm_s, past, dt)

    w_bf = w_in[0].astype(BF16)
    gn = g_norm[0][None, :]
    ggn = g_ret_gn[0][None, :]
    wo_bf = w_out[0].astype(BF16)
    wp_bf = w_ple[0].astype(BF16)
    wg_bf = w_ple_gate[0].astype(BF16)
    gp = g_ple[0][None, :]
    gf = g_final[None, :]

    xp = x_prompt.reshape(b * t, d)
    rq, rk, rv, rg, sqb, sk, sv, sg, skb, svb = _project(xp, gn, w_bf, cos_p, sin_p, t // tm_p, tm_p)
    yr, s_fin = _retention_prompt(rq, rk, rv, rg, ggn, b, t)
    tq = min(1024, t)
    ys = _sb_prompt(sb_bias[0], sqb, skb, svb, sg, b, t, tq, min(256, tq))
    y_prompt = _out_proj(yr.reshape(b * t, SEG_W), ys, xp, p_prompt[0].reshape(b * t, -1),
                         wo_bf, wp_bf, wg_bf, gp, gf, min(512, b * t))

    xs = x_sample.reshape(db * dt, d)
    rq_s, rk_s, rv_s, rg_s, sqb_s, sk_s, sv_s, sg_s, skb_s, svb_s = _project(xs, gn, w_bf, cos_s, sin_s, 1, tm_s)
    yr_s, s_new = _retention_sample(rq_s, rk_s, rv_s, rg_s, ggn, state_ret[0], dt)
    q_rep = jnp.repeat(sqb_s.reshape(db, dt, SEG_W), SB_HEADS, axis=1)
    pad = ((0, 0), (0, 0), (0, page - dt))
    k_new = jnp.pad(jnp.swapaxes(skb_s.reshape(db, dt, SEG_W), 1, 2), pad)
    v_new = jnp.pad(jnp.swapaxes(svb_s.reshape(db, dt, SEG_W), 1, 2), pad)
    bias_col = jnp.tile(sb_bias[0] * LOG2E, dt)[:, None]
    cache_kt = jnp.transpose(cache_sb_k[0], (0, 2, 3, 1)).reshape(n_pool, SEG_W, page)
    cache_vt = jnp.transpose(cache_sb_v[0], (0, 2, 3, 1)).reshape(n_pool, SEG_W, page)
    ys_s = _sb_sample(page_table, bias_col, q_rep, k_new, v_new, sg_s.reshape(db, dt, SEG_W),
                      cache_kt, cache_vt, 8)
    y_sample = _out_proj(yr_s, ys_s.reshape(db * dt, SEG_W), xs, p_sample[0].reshape(db * dt, -1),
                         wo_bf, wp_bf, wg_bf, gp, gf, min(512, db * dt))

    return (y_prompt.reshape(b, t, d),
            y_sample.reshape(db, dt, d),
            sk.reshape(1, b, t, SB_HEADS, SB_DH),
            sv.reshape(1, b, t, SB_HEADS, SB_DH),
            sk_s.reshape(1, db, dt, SB_HEADS, SB_DH),
            sv_s.reshape(1, db, dt, SB_HEADS, SB_DH),
            s_fin.reshape(1, b, RET_HEADS, RET_D, RET_D),
            s_new[None])
```

```python
import functools
import math

import numpy as np
import jax
import jax.numpy as jnp
from jax import lax
from jax.experimental import pallas as pl
from jax.experimental.pallas import tpu as pltpu

F32 = jnp.float32
BF16 = jnp.bfloat16

RET_HEADS = 4
RET_D = 128
SB_HEADS = 8
SB_DH = 64
SEG_W = 512
CHUNK = 128
ROPE_BASE = 10000.0
EPS = 1e-6
LOG2E = math.log2(math.e)
LANES = 128
VMEM_LIMIT = 48 * 1024 * 1024

LOG_G = [float(np.log(np.float32(1.0) - np.float32(2.0) ** np.float32(-5.0 - h))) for h in range(RET_HEADS)]


def _nt_dot(a, b):
    return lax.dot_general(a, b, (((1,), (1,)), ((), ())), preferred_element_type=F32)


def _tn_dot(a, b):
    return lax.dot_general(a, b, (((0,), (0,)), ((), ())), preferred_element_type=F32)


def _dot(a, b):
    return jnp.dot(a, b, preferred_element_type=F32)


def _silu(x):
    return x / (1.0 + jnp.exp(-x))


def _rope_kernel(inv_ref, cos_ref, sin_ref, *, base, period, rows):
    r = lax.broadcasted_iota(jnp.int32, (rows, LANES), 0) + pl.program_id(0) * rows
    lane = lax.broadcasted_iota(jnp.int32, (rows, LANES), 1)
    pos = (base + r % period).astype(F32)
    ang = pos * inv_ref[...]
    cos_ref[...] = jnp.cos(ang)
    sin_ref[...] = jnp.where(lane < LANES // 2, -1.0, 1.0) * jnp.sin(ang)


def _rope_table(inv128, n_rows, base, period):
    rows = min(n_rows, 512)
    return pl.pallas_call(
        functools.partial(_rope_kernel, base=base, period=period, rows=rows),
        grid=(n_rows // rows,),
        in_specs=[pl.BlockSpec((1, LANES), lambda i: (0, 0))],
        out_specs=[pl.BlockSpec((rows, LANES), lambda i: (i, 0))] * 2,
        out_shape=[jax.ShapeDtypeStruct((n_rows, LANES), F32)] * 2,
        name="rope_table",
    )(inv128)


def _proj_kernel(x_ref, g_ref, w_ref, cos_ref, sin_ref,
                 rq_ref, rk_ref, rv_ref, rg_ref, sqb_ref, sk_ref, sv_ref, sg_ref, skb_ref, svb_ref):
    x = x_ref[...]
    ms = jnp.mean(x * x, axis=-1, keepdims=True)
    xn = ((x * lax.rsqrt(ms + EPS)) * g_ref[...]).astype(BF16)
    cos = cos_ref[...]
    sin = sin_ref[...]

    def seg(j):
        return _dot(xn, w_ref[:, j * SEG_W:(j + 1) * SEG_W])

    def rotary_store(p, out_ref, scale):
        for h in range(RET_HEADS):
            ph = p[:, h * RET_D:(h + 1) * RET_D]
            rot = ph * cos + pltpu.roll(ph, RET_D // 2, axis=1) * sin
            out_ref[:, h * RET_D:(h + 1) * RET_D] = rot if scale is None else rot * scale

    rotary_store(seg(0), rq_ref, None)
    rotary_store(seg(1), rk_ref, RET_D ** -0.5)
    rv_ref[...] = seg(2)
    rg_ref[...] = seg(3)
    sqb_ref[...] = (seg(4) * (LOG2E * SB_DH ** -0.5)).astype(BF16)
    sk = seg(5)
    sk_ref[...] = sk
    skb_ref[...] = sk.astype(BF16)
    sv = seg(6)
    sv_ref[...] = sv
    svb_ref[...] = sv.astype(BF16)
    sg_ref[...] = seg(7)


def _project(x2d, g_norm, w_bf, cos_t, sin_t, table_period_blocks, tm):
    n, d = x2d.shape
    d_in = w_bf.shape[1]
    row = lambda i: (i, 0)
    f32_out = jax.ShapeDtypeStruct((n, SEG_W), F32)
    bf_out = jax.ShapeDtypeStruct((n, SEG_W), BF16)
    seg_spec = pl.BlockSpec((tm, SEG_W), row)
    tab_spec = pl.BlockSpec((tm, LANES), lambda i: (i % table_period_blocks, 0))
    return pl.pallas_call(
        _proj_kernel,
        grid=(n // tm,),
        in_specs=[pl.BlockSpec((tm, d), row),
                  pl.BlockSpec((1, d), lambda i: (0, 0)),
                  pl.BlockSpec((d, d_in), lambda i: (0, 0)),
                  tab_spec, tab_spec],
        out_specs=[seg_spec] * 10,
        out_shape=[f32_out, f32_out, f32_out, f32_out, bf_out, f32_out, f32_out, f32_out, bf_out, bf_out],
        compiler_params=pltpu.CompilerParams(dimension_semantics=("parallel",), vmem_limit_bytes=VMEM_LIMIT),
        name="rmsnorm_in_proj",
    )(x2d, g_norm, w_bf, cos_t, sin_t)


def _group_norm_gate(o, gn, rg):
    mu = jnp.mean(o, axis=-1, keepdims=True)
    d = o - mu
    var = jnp.mean(d * d, axis=-1, keepdims=True)
    return (d * lax.rsqrt(var + EPS)) * gn * _silu(rg)


def _ret_prompt_kernel(q_ref, k_ref, v_ref, rg_ref, gn_ref, y_ref, sfin_ref,
                       state_ref, decay_ref, qdec_ref, kdec_ref, *, batch, chunk):
    c = pl.program_id(0)

    @pl.when(c == 0)
    def _init():
        state_ref[...] = jnp.zeros_like(state_ref)
        n = lax.broadcasted_iota(jnp.int32, (chunk, chunk), 0).astype(F32)
        m = lax.broadcasted_iota(jnp.int32, (chunk, chunk), 1).astype(F32)
        diff = n - m
        causal = diff >= 0
        for h in range(RET_HEADS):
            decay_ref[h] = jnp.where(causal, jnp.exp(jnp.where(causal, diff, 0.0) * LOG_G[h]), 0.0)
            qdec_ref[h] = jnp.exp((n + 1.0) * LOG_G[h])
            kdec_ref[h] = jnp.exp((chunk - 1.0 - n) * LOG_G[h])

    for b in range(batch):
        for h in range(RET_HEADS):
            cols = slice(h * RET_D, (h + 1) * RET_D)
            q = q_ref[b, :, cols]
            k = k_ref[b, :, cols]
            vb = v_ref[b, :, cols].astype(BF16)
            s = state_ref[b * RET_HEADS + h]
            inner = _nt_dot(q.astype(BF16), k.astype(BF16)) * decay_ref[h]
            o = _dot(inner.astype(BF16), vb) + _dot((q * qdec_ref[h]).astype(BF16), s.astype(BF16))
            state_ref[b * RET_HEADS + h] = (math.exp(chunk * LOG_G[h]) * s
                                            + _tn_dot((k * kdec_ref[h]).astype(BF16), vb))
            y_ref[b, :, cols] = _group_norm_gate(o, gn_ref[:, cols], rg_ref[b, :, cols])

    @pl.when(c == pl.num_programs(0) - 1)
    def _fin():
        sfin_ref[...] = state_ref[...]


def _retention_prompt(rq, rk, rv, rg, gn, batch, seq):
    nc = seq // CHUNK
    blk = pl.BlockSpec((batch, CHUNK, SEG_W), lambda c: (0, c, 0))
    sq_mat = pltpu.VMEM((RET_HEADS, CHUNK, CHUNK), F32)
    return pl.pallas_call(
        functools.partial(_ret_prompt_kernel, batch=batch, chunk=CHUNK),
        grid=(nc,),
        in_specs=[blk, blk, blk, blk, pl.BlockSpec((1, SEG_W), lambda c: (0, 0))],
        out_specs=[blk, pl.BlockSpec((batch * RET_HEADS, RET_D, RET_D), lambda c: (0, 0, 0))],
        out_shape=[jax.ShapeDtypeStruct((batch, seq, SEG_W), F32),
                   jax.ShapeDtypeStruct((batch * RET_HEADS, RET_D, RET_D), F32)],
        scratch_shapes=[pltpu.VMEM((batch * RET_HEADS, RET_D, RET_D), F32), sq_mat, sq_mat, sq_mat],
        compiler_params=pltpu.CompilerParams(dimension_semantics=("arbitrary",), vmem_limit_bytes=VMEM_LIMIT),
        name="retention_prompt",
    )(rq.reshape(batch, seq, SEG_W), rk.reshape(batch, seq, SEG_W), rv.reshape(batch, seq, SEG_W),
      rg.reshape(batch, seq, SEG_W), gn)


def _ret_sample_kernel(q_ref, k_ref, v_ref, rg_ref, gn_ref, s0_ref, y_ref, s1_ref, *, bb, dt):
    grp = 16 // dt
    rows = grp * dt
    ri = lax.broadcasted_iota(jnp.int32, (rows, rows), 0)
    ci = lax.broadcasted_iota(jnp.int32, (rows, rows), 1)
    diff = (ri % dt - ci % dt).astype(F32)
    keep = (ri // dt == ci // dt) & (diff >= 0)
    rn = (lax.broadcasted_iota(jnp.int32, (rows, RET_D), 0) % dt).astype(F32)
    rb = lax.broadcasted_iota(jnp.int32, (rows, RET_D), 0) // dt
    for g in range(bb // grp):
        rsl = slice(g * rows, (g + 1) * rows)
        for h in range(RET_HEADS):
            cols = slice(h * RET_D, (h + 1) * RET_D)
            lg = LOG_G[h]
            decay = jnp.where(keep, jnp.exp(jnp.where(keep, diff, 0.0) * lg), 0.0)
            q = q_ref[rsl, cols]
            k = k_ref[rsl, cols]
            vb = v_ref[rsl, cols].astype(BF16)
            inner = _nt_dot(q.astype(BF16), k.astype(BF16)) * decay
            o = _dot(inner.astype(BF16), vb)
            qd = (q * jnp.exp((rn + 1.0) * lg)).astype(BF16)
            kd = k * jnp.exp((dt - 1.0 - rn) * lg)
            for j in range(grp):
                b = g * grp + j
                s0 = s0_ref[b, h]
                o = o + jnp.where(rb == j, _dot(qd, s0.astype(BF16)), 0.0)
                kdj = jnp.where(rb == j, kd, 0.0).astype(BF16)
                s1_ref[b, h] = math.exp(dt * lg) * s0 + _tn_dot(kdj, vb)
            y_ref[rsl, cols] = _group_norm_gate(o, gn_ref[:, cols], rg_ref[rsl, cols])


def _retention_sample(rq, rk, rv, rg, gn, state, dt):
    n = rq.shape[0]
    db = n // dt
    bb = 8
    blk = pl.BlockSpec((bb * dt, SEG_W), lambda i: (i, 0))
    st = pl.BlockSpec((bb, RET_HEADS, RET_D, RET_D), lambda i: (i, 0, 0, 0))
    return pl.pallas_call(
        functools.partial(_ret_sample_kernel, bb=bb, dt=dt),
        grid=(db // bb,),
        in_specs=[blk, blk, blk, blk, pl.BlockSpec((1, SEG_W), lambda i: (0, 0)), st],
        out_specs=[blk, st],
        out_shape=[jax.ShapeDtypeStruct((n, SEG_W), F32), jax.ShapeDtypeStruct(state.shape, F32)],
        compiler_params=pltpu.CompilerParams(dimension_semantics=("parallel",), vmem_limit_bytes=VMEM_LIMIT),
        name="retention_sample",
    )(rq, rk, rv, rg, gn, state)


def _tri(n):
    r = lax.broadcasted_iota(jnp.int32, (n, n), 0)
    c = lax.broadcasted_iota(jnp.int32, (n, n), 1)
    return jnp.where(r > c, 1.0, 0.0).astype(BF16)


def _softplus2(z, mask):
    neg_abs = lax.bitcast_convert_type(lax.bitcast_convert_type(z, jnp.uint32) | jnp.uint32(0x80000000), F32)
    sp = jnp.maximum(z, 0.0) + jnp.log2(1.0 + jnp.exp2(neg_abs))
    return sp if mask is None else jnp.where(mask, sp, 0.0)


def _sp_suffix(sp, tri):
    return sp + _dot(sp.astype(BF16), tri)


def _sb_weights(z, suffix, carry, mask):
    a = jnp.exp2(z - (suffix + carry))
    return a if mask is None else jnp.where(mask, a, 0.0)


def _sb_prompt_kernel(bias_ref, q_ref, k_ref, v_ref, sg_ref, o_ref, qz_ref, acc_ref, carry_ref, *, tq, tk):
    hp = pl.program_id(1)
    qi = pl.program_id(2)
    nd = tq // tk
    lane = lax.broadcasted_iota(jnp.int32, (tq, LANES), 1)
    q = q_ref[...]
    qz_ref[0] = jnp.where(lane < SB_DH, q, jnp.zeros_like(q))
    qz_ref[1] = jnp.where(lane >= SB_DH, q, jnp.zeros_like(q))
    acc_ref[...] = jnp.zeros_like(acc_ref)
    carry_ref[...] = jnp.zeros_like(carry_ref)
    tri = _tri(tk)
    bias = [bias_ref[2 * hp] * LOG2E, bias_ref[2 * hp + 1] * LOG2E]
    q0 = qi * tq

    def block(k0, r0, mask):
        kb = k_ref[pl.ds(k0, tk), :]
        vb = v_ref[pl.ds(k0, tk), :]
        for h in range(2):
            z = _nt_dot(qz_ref[h, r0:, :], kb) + bias[h]
            suffix = _sp_suffix(_softplus2(z, mask), tri)
            carry = carry_ref[h, r0:, :]
            a = _sb_weights(z, suffix, carry, mask)
            carry_ref[h, r0:, :] = carry + suffix[:, 0:1]
            acc_ref[h, r0:, :] += _dot(a.astype(BF16), vb)

    for d in reversed(range(nd)):
        rl = lax.broadcasted_iota(jnp.int32, (tq - d * tk, tk), 0)
        cl = lax.broadcasted_iota(jnp.int32, (tq - d * tk, tk), 1)
        block(pl.multiple_of(q0 + d * tk, tk), d * tk, cl < rl)

    def body(i, _):
        block(pl.multiple_of(q0 - (i + 1) * tk, tk), 0, None)
        return 0

    lax.fori_loop(0, qi * nd, body, 0)
    o = jnp.where(lane < SB_DH, acc_ref[0], acc_ref[1])
    o_ref[...] = o * _silu(sg_ref[...])


def _sb_prompt(sb_bias, sqb, skb, svb, sg, batch, seq, tq, tk):
    nq = seq // tq
    qblk = pl.BlockSpec((tq, LANES), lambda b, hp, i: (b * nq + i, hp))
    kvblk = pl.BlockSpec((seq, LANES), lambda b, hp, i: (b, hp))
    return pl.pallas_call(
        functools.partial(_sb_prompt_kernel, tq=tq, tk=tk),
        grid=(batch, SB_HEADS // 2, nq),
        in_specs=[pl.BlockSpec(memory_space=pltpu.SMEM), qblk, kvblk, kvblk, qblk],
        out_specs=qblk,
        out_shape=jax.ShapeDtypeStruct((batch * seq, SEG_W), F32),
        scratch_shapes=[pltpu.VMEM((2, tq, LANES), BF16), pltpu.VMEM((2, tq, LANES), F32),
                        pltpu.VMEM((2, tq, 1), F32)],
        compiler_params=pltpu.CompilerParams(dimension_semantics=("parallel", "parallel", "arbitrary"),
                                             vmem_limit_bytes=VMEM_LIMIT),
        name="sb_prompt",
    )(sb_bias, sqb, skb, svb, sg)


def _sb_sample_kernel(pt_ref, bias_ref, q_ref, kn_ref, vn_ref, sg_ref, *rest, pp, page, dt):
    k_refs = rest[:pp]
    v_refs = rest[pp:2 * pp]
    o_ref, qbd_ref, acc_ref, carry_ref = rest[2 * pp:]
    j = pl.program_id(1)
    rows = dt * SB_HEADS
    tri = _tri(page)
    bias = bias_ref[...]

    def pages(kts, vts, mask):
        qbd = qbd_ref[...]
        zs = [_dot(qbd, kt) + bias for kt in kts]
        suffixes = [_sp_suffix(_softplus2(z, mask), tri) for z in zs]
        carry = carry_ref[...]
        acc = acc_ref[...]
        for z, suffix, vt in zip(zs, suffixes, vts):
            a = _sb_weights(z, suffix, carry, mask)
            acc = acc + _nt_dot(a.astype(BF16), vt)
            carry = carry + suffix[:, 0:1]
        carry_ref[...] = carry
        acc_ref[...] = acc

    @pl.when(j == 0)
    def _start():
        r = lax.broadcasted_iota(jnp.int32, (rows, SEG_W), 0)
        c = lax.broadcasted_iota(jnp.int32, (rows, SEG_W), 1)
        q = q_ref[0]
        qbd_ref[...] = jnp.where(c // SB_DH == r % SB_HEADS, q, jnp.zeros_like(q))
        acc_ref[...] = jnp.zeros_like(acc_ref)
        carry_ref[...] = jnp.zeros_like(carry_ref)
        rq = lax.broadcasted_iota(jnp.int32, (rows, page), 0) // SB_HEADS
        cm = lax.broadcasted_iota(jnp.int32, (rows, page), 1)
        pages([kn_ref[0]], [vn_ref[0]], cm < rq)

    order = list(reversed(range(pp)))
    pages([k_refs[i][0].astype(BF16) for i in order], [v_refs[i][0].astype(BF16) for i in order], None)

    @pl.when(j == pl.num_programs(1) - 1)
    def _fin():
        r = lax.broadcasted_iota(jnp.int32, (rows, SEG_W), 0)
        c = lax.broadcasted_iota(jnp.int32, (rows, SEG_W), 1)
        own = jnp.where(c // SB_DH == r % SB_HEADS, acc_ref[...], 0.0)
        o = jnp.sum(own.reshape(dt, SB_HEADS, SEG_W), axis=1)
        o_ref[0] = o * _silu(sg_ref[0])


def _sb_sample(page_table, bias_col, q_rep, k_new, v_new, sg, cache_k, cache_v, pp):
    db, n_pages = page_table.shape
    page = cache_k.shape[2]
    dt = sg.shape[1]
    rows = dt * SB_HEADS
    nstep = n_pages // pp
    per_b = lambda blk: pl.BlockSpec((1,) + blk, lambda b, j, pt: (b, 0, 0))

    def page_spec(i):
        return pl.BlockSpec((1, SEG_W, page), lambda b, j, pt: (pt[b, (nstep - 1 - j) * pp + i], 0, 0))

    grid_spec = pltpu.PrefetchScalarGridSpec(
        num_scalar_prefetch=1,
        grid=(db, nstep),
        in_specs=[pl.BlockSpec((rows, 1), lambda b, j, pt: (0, 0)),
                  per_b((rows, SEG_W)), per_b((SEG_W, page)), per_b((SEG_W, page)), per_b((dt, SEG_W))]
                 + [page_spec(i) for i in range(pp)] * 2,
        out_specs=per_b((dt, SEG_W)),
        scratch_shapes=[pltpu.VMEM((rows, SEG_W), BF16), pltpu.VMEM((rows, SEG_W), F32),
                        pltpu.VMEM((rows, 1), F32)],
    )
    return pl.pallas_call(
        functools.partial(_sb_sample_kernel, pp=pp, page=page, dt=dt),
        grid_spec=grid_spec,
        out_shape=jax.ShapeDtypeStruct((db, dt, SEG_W), F32),
        compiler_params=pltpu.CompilerParams(dimension_semantics=("parallel", "arbitrary"),
                                             vmem_limit_bytes=VMEM_LIMIT),
        name="sb_sample",
    )(page_table, bias_col, q_rep, k_new, v_new, sg, *([cache_k] * pp), *([cache_v] * pp))


def _out_kernel(yr_ref, ys_ref, h_ref, p_ref, wo_ref, wp_ref, wg_ref, gp_ref, gf_ref, o_ref):
    y = _dot(yr_ref[...].astype(BF16), wo_ref[:SEG_W, :]) + _dot(ys_ref[...].astype(BF16), wo_ref[SEG_W:, :])
    h1 = h_ref[...] + y
    e = _dot(p_ref[...].astype(BF16), wp_ref[...])
    e = (e * lax.rsqrt(jnp.mean(e * e, axis=-1, keepdims=True) + EPS)) * gp_ref[...]
    gate = 1.0 / (1.0 + jnp.exp(-_dot(h1.astype(BF16), wg_ref[...])))
    h2 = h1 + e * gate
    o_ref[...] = (h2 * lax.rsqrt(jnp.mean(h2 * h2, axis=-1, keepdims=True) + EPS)) * gf_ref[...]


def _out_proj(yr, ys, h, p, wo_bf, wp_bf, wg_bf, g_ple, g_final, tm):
    n, d = h.shape
    d_ple = p.shape[1]
    row = lambda i: (i, 0)
    const = lambda i: (0, 0)
    return pl.pallas_call(
        _out_kernel,
        grid=(n // tm,),
        in_specs=[pl.BlockSpec((tm, SEG_W), row), pl.BlockSpec((tm, SEG_W), row), pl.BlockSpec((tm, d), row),
                  pl.BlockSpec((tm, d_ple), row), pl.BlockSpec((2 * SEG_W, d), const),
                  pl.BlockSpec((d_ple, d), const), pl.BlockSpec((d, d), const),
                  pl.BlockSpec((1, d), const), pl.BlockSpec((1, d), const)],
        out_specs=pl.BlockSpec((tm, d), row),
        out_shape=jax.ShapeDtypeStruct((n, d), F32),
        compiler_params=pltpu.CompilerParams(dimension_semantics=("parallel",), vmem_limit_bytes=VMEM_LIMIT),
        name="out_proj_ple_norm",
    )(yr, ys, h, p, wo_bf, wp_bf, wg_bf, g_ple, g_final)


def kernel(x_prompt, x_sample, cache_sb_k, cache_sb_v, state_ret, page_table, p_prompt, p_sample,
           g_norm, w_in, sb_bias, g_ret_gn, w_out, w_ple, g_ple, w_ple_gate, g_final):
    b, t, d = x_prompt.shape
    db, dt, _ = x_sample.shape
    depth = w_in.shape[0]
    assert depth == 1, "single-layer trunk"
    n_pool, page = cache_sb_k.shape[1], cache_sb_k.shape[2]
    past = page_table.shape[1] * page
    tm_p = min(256, t)
    tm_s = min(256, db * dt)

    half = RET_D // 2
    inv = ROPE_BASE ** (-jnp.arange(half, dtype=F32) / half)
    inv128 = jnp.concatenate([inv, inv])[None, :]
    cos_p, sin_p = _rope_table(inv128, t, 0, t)
    cos_s, sin_s = _rope_table(inv128, tm_s, past, dt)

    w_bf = w_in[0].astype(BF16)
    gn = g_norm[0][None, :]
    ggn = g_ret_gn[0][None, :]
    wo_bf = w_out[0].astype(BF16)
    wp_bf = w_ple[0].astype(BF16)
    wg_bf = w_ple_gate[0].astype(BF16)
    gp = g_ple[0][None, :]
    gf = g_final[None, :]

    xp = x_prompt.reshape(b * t, d)
    rq, rk, rv, rg, sqb, sk, sv, sg, skb, svb = _project(xp, gn, w_bf, cos_p, sin_p, t // tm_p, tm_p)
    yr, s_fin = _retention_prompt(rq, rk, rv, rg, ggn, b, t)
    tq = min(2048, t)
    ys = _sb_prompt(sb_bias[0], sqb, skb, svb, sg, b, t, tq, min(256, tq))
    y_prompt = _out_proj(yr.reshape(b * t, SEG_W), ys, xp, p_prompt[0].reshape(b * t, -1),
                         wo_bf, wp_bf, wg_bf, gp, gf, min(512, b * t))

    xs = x_sample.reshape(db * dt, d)
    rq_s, rk_s, rv_s, rg_s, sqb_s, sk_s, sv_s, sg_s, skb_s, svb_s = _project(xs, gn, w_bf, cos_s, sin_s, 1, tm_s)
    yr_s, s_new = _retention_sample(rq_s, rk_s, rv_s, rg_s, ggn, state_ret[0], dt)
    q_rep = jnp.repeat(sqb_s.reshape(db, dt, SEG_W), SB_HEADS, axis=1)
    pad = ((0, 0), (0, 0), (0, page - dt))
    k_new = jnp.pad(jnp.swapaxes(skb_s.reshape(db, dt, SEG_W), 1, 2), pad)
    v_new = jnp.pad(jnp.swapaxes(svb_s.reshape(db, dt, SEG_W), 1, 2), pad)
    bias_col = jnp.tile(sb_bias[0] * LOG2E, dt)[:, None]
    cache_kt = jnp.transpose(cache_sb_k[0], (0, 2, 3, 1)).reshape(n_pool, SEG_W, page)
    cache_vt = jnp.transpose(cache_sb_v[0], (0, 2, 3, 1)).reshape(n_pool, SEG_W, page)
    ys_s = _sb_sample(page_table, bias_col, q_rep, k_new, v_new, sg_s.reshape(db, dt, SEG_W),
                      cache_kt, cache_vt, 8)
    y_sample = _out_proj(yr_s, ys_s.reshape(db * dt, SEG_W), xs, p_sample[0].reshape(db * dt, -1),
                         wo_bf, wp_bf, wg_bf, gp, gf, min(512, db * dt))

    return (y_prompt.reshape(b, t, d),
            y_sample.reshape(db, dt, d),
            sk.reshape(1, b, t, SB_HEADS, SB_DH),
            sv.reshape(1, b, t, SB_HEADS, SB_DH),
            sk_s.reshape(1, db, dt, SB_HEADS, SB_DH),
            sv_s.reshape(1, db, dt, SB_HEADS, SB_DH),
            s_fin.reshape(1, b, RET_HEADS, RET_D, RET_D),
            s_new[None])
```

```python
import functools
import math

import numpy as np
import jax
import jax.numpy as jnp
from jax import lax
from jax.experimental import pallas as pl
from jax.experimental.pallas import tpu as pltpu

F32 = jnp.float32
BF16 = jnp.bfloat16

RET_HEADS = 4
RET_D = 128
SB_HEADS = 8
SB_DH = 64
SEG_W = 512
CHUNK = 128
ROPE_BASE = 10000.0
EPS = 1e-6
LOG2E = math.log2(math.e)
LANES = 128
VMEM_LIMIT = 48 * 1024 * 1024

LOG_G = [float(np.log(np.float32(1.0) - np.float32(2.0) ** np.float32(-5.0 - h))) for h in range(RET_HEADS)]


def _nt_dot(a, b):
    return lax.dot_general(a, b, (((1,), (1,)), ((), ())), preferred_element_type=F32)


def _tn_dot(a, b):
    return lax.dot_general(a, b, (((0,), (0,)), ((), ())), preferred_element_type=F32)


def _dot(a, b):
    return jnp.dot(a, b, preferred_element_type=F32)


def _silu(x):
    return x / (1.0 + jnp.exp(-x))


def _rope_kernel(inv_ref, cos_ref, sin_ref, *, base, period, rows):
    r = lax.broadcasted_iota(jnp.int32, (rows, LANES), 0) + pl.program_id(0) * rows
    lane = lax.broadcasted_iota(jnp.int32, (rows, LANES), 1)
    pos = (base + r % period).astype(F32)
    ang = pos * inv_ref[...]
    cos_ref[...] = jnp.cos(ang)
    sin_ref[...] = jnp.where(lane < LANES // 2, -1.0, 1.0) * jnp.sin(ang)


def _rope_table(inv128, n_rows, base, period):
    rows = min(n_rows, 512)
    return pl.pallas_call(
        functools.partial(_rope_kernel, base=base, period=period, rows=rows),
        grid=(n_rows // rows,),
        in_specs=[pl.BlockSpec((1, LANES), lambda i: (0, 0))],
        out_specs=[pl.BlockSpec((rows, LANES), lambda i: (i, 0))] * 2,
        out_shape=[jax.ShapeDtypeStruct((n_rows, LANES), F32)] * 2,
        name="rope_table",
    )(inv128)


def _proj_kernel(x_ref, g_ref, w_ref, wkv_t_ref, cos_ref, sin_ref,
                 rq_ref, rk_ref, rv_ref, rg_ref, sqb_ref, sk_ref, sv_ref, sg_ref, skb_ref, svb_ref, *, kv_transposed):
    x = x_ref[...]
    ms = jnp.mean(x * x, axis=-1, keepdims=True)
    xn = ((x * lax.rsqrt(ms + EPS)) * g_ref[...]).astype(BF16)
    cos = cos_ref[...]
    sin = sin_ref[...]

    def seg(j):
        return _dot(xn, w_ref[:, j * SEG_W:(j + 1) * SEG_W])

    def rotary_store(p, out_ref, scale):
        for h in range(RET_HEADS):
            ph = p[:, h * RET_D:(h + 1) * RET_D]
            rot = ph * cos + pltpu.roll(ph, RET_D // 2, axis=1) * sin
            out_ref[:, h * RET_D:(h + 1) * RET_D] = rot if scale is None else rot * scale

    rotary_store(seg(0), rq_ref, None)
    rotary_store(seg(1), rk_ref, RET_D ** -0.5)
    rv_ref[...] = seg(2)
    rg_ref[...] = seg(3)
    sqb_ref[...] = (seg(4) * (LOG2E * SB_DH ** -0.5)).astype(BF16)
    if kv_transposed:
        sk_t = _nt_dot(wkv_t_ref[0], xn)
        sk_ref[0] = sk_t
        skb_ref[0] = sk_t.astype(BF16)
        sv_ref[0] = _nt_dot(wkv_t_ref[1], xn)
        svb_ref[...] = seg(6).astype(BF16)
    else:
        sk = seg(5)
        sk_ref[...] = sk
        skb_ref[...] = sk.astype(BF16)
        sv = seg(6)
        sv_ref[...] = sv
        svb_ref[...] = sv.astype(BF16)
    sg_ref[...] = seg(7)


def _project(x2d, g_norm, w_bf, wkv_t, cos_t, sin_t, table_period_blocks, tm, seq=None):
    n, d = x2d.shape
    d_in = w_bf.shape[1]
    row = lambda i: (i, 0)
    const2 = lambda i: (0, 0)
    f32_out = jax.ShapeDtypeStruct((n, SEG_W), F32)
    bf_out = jax.ShapeDtypeStruct((n, SEG_W), BF16)
    seg_spec = pl.BlockSpec((tm, SEG_W), row)
    tab_spec = pl.BlockSpec((tm, LANES), lambda i: (i % table_period_blocks, 0))
    if seq is None:
        kv_f32, kv_bf, kv_spec = f32_out, bf_out, seg_spec
    else:
        per_b = seq // tm
        kv_f32 = jax.ShapeDtypeStruct((n // seq, SEG_W, seq), F32)
        kv_bf = jax.ShapeDtypeStruct((n // seq, SEG_W, seq), BF16)
        kv_spec = pl.BlockSpec((1, SEG_W, tm), lambda i: (i // per_b, 0, i % per_b))
    return pl.pallas_call(
        functools.partial(_proj_kernel, kv_transposed=seq is not None),
        grid=(n // tm,),
        in_specs=[pl.BlockSpec((tm, d), row),
                  pl.BlockSpec((1, d), const2),
                  pl.BlockSpec((d, d_in), const2),
                  pl.BlockSpec((2, SEG_W, d), lambda i: (0, 0, 0)),
                  tab_spec, tab_spec],
        out_specs=[seg_spec, seg_spec, seg_spec, seg_spec, seg_spec, kv_spec, kv_spec, seg_spec, kv_spec, seg_spec],
        out_shape=[f32_out, f32_out, f32_out, f32_out, bf_out, kv_f32, kv_f32, f32_out, kv_bf, bf_out],
        compiler_params=pltpu.CompilerParams(dimension_semantics=("parallel",), vmem_limit_bytes=VMEM_LIMIT),
        name="rmsnorm_in_proj",
    )(x2d, g_norm, w_bf, wkv_t, cos_t, sin_t)


def _group_norm_gate(o, gn, rg):
    mu = jnp.mean(o, axis=-1, keepdims=True)
    d = o - mu
    var = jnp.mean(d * d, axis=-1, keepdims=True)
    return (d * lax.rsqrt(var + EPS)) * gn * _silu(rg)


def _ret_prompt_kernel(q_ref, k_ref, v_ref, rg_ref, gn_ref, y_ref, sfin_ref,
                       state_ref, decay_ref, qdec_ref, kdec_ref, *, batch, chunk):
    c = pl.program_id(0)

    @pl.when(c == 0)
    def _init():
        state_ref[...] = jnp.zeros_like(state_ref)
        n = lax.broadcasted_iota(jnp.int32, (chunk, chunk), 0).astype(F32)
        m = lax.broadcasted_iota(jnp.int32, (chunk, chunk), 1).astype(F32)
        diff = n - m
        causal = diff >= 0
        for h in range(RET_HEADS):
            decay_ref[h] = jnp.where(causal, jnp.exp(jnp.where(causal, diff, 0.0) * LOG_G[h]), 0.0)
            qdec_ref[h] = jnp.exp((n + 1.0) * LOG_G[h])
            kdec_ref[h] = jnp.exp((chunk - 1.0 - n) * LOG_G[h])

    for b in range(batch):
        for h in range(RET_HEADS):
            cols = slice(h * RET_D, (h + 1) * RET_D)
            q = q_ref[b, :, cols]
            k = k_ref[b, :, cols]
            vb = v_ref[b, :, cols].astype(BF16)
            s = state_ref[b * RET_HEADS + h]
            inner = _nt_dot(q.astype(BF16), k.astype(BF16)) * decay_ref[h]
            o = _dot(inner.astype(BF16), vb) + _dot((q * qdec_ref[h]).astype(BF16), s.astype(BF16))
            state_ref[b * RET_HEADS + h] = (math.exp(chunk * LOG_G[h]) * s
                                            + _tn_dot((k * kdec_ref[h]).astype(BF16), vb))
            y_ref[b, :, cols] = _group_norm_gate(o, gn_ref[:, cols], rg_ref[b, :, cols])

    @pl.when(c == pl.num_programs(0) - 1)
    def _fin():
        sfin_ref[...] = state_ref[...]


def _retention_prompt(rq, rk, rv, rg, gn, batch, seq):
    nc = seq // CHUNK
    blk = pl.BlockSpec((batch, CHUNK, SEG_W), lambda c: (0, c, 0))
    sq_mat = pltpu.VMEM((RET_HEADS, CHUNK, CHUNK), F32)
    return pl.pallas_call(
        functools.partial(_ret_prompt_kernel, batch=batch, chunk=CHUNK),
        grid=(nc,),
        in_specs=[blk, blk, blk, blk, pl.BlockSpec((1, SEG_W), lambda c: (0, 0))],
        out_specs=[blk, pl.BlockSpec((batch * RET_HEADS, RET_D, RET_D), lambda c: (0, 0, 0))],
        out_shape=[jax.ShapeDtypeStruct((batch, seq, SEG_W), F32),
                   jax.ShapeDtypeStruct((batch * RET_HEADS, RET_D, RET_D), F32)],
        scratch_shapes=[pltpu.VMEM((batch * RET_HEADS, RET_D, RET_D), F32), sq_mat, sq_mat, sq_mat],
        compiler_params=pltpu.CompilerParams(dimension_semantics=("arbitrary",), vmem_limit_bytes=VMEM_LIMIT),
        name="retention_prompt",
    )(rq.reshape(batch, seq, SEG_W), rk.reshape(batch, seq, SEG_W), rv.reshape(batch, seq, SEG_W),
      rg.reshape(batch, seq, SEG_W), gn)


def _ret_sample_kernel(q_ref, k_ref, v_ref, rg_ref, gn_ref, s0_ref, y_ref, s1_ref, *, bb, dt):
    grp = 16 // dt
    rows = grp * dt
    ri = lax.broadcasted_iota(jnp.int32, (rows, rows), 0)
    ci = lax.broadcasted_iota(jnp.int32, (rows, rows), 1)
    diff = (ri % dt - ci % dt).astype(F32)
    keep = (ri // dt == ci // dt) & (diff >= 0)
    rn = (lax.broadcasted_iota(jnp.int32, (rows, RET_D), 0) % dt).astype(F32)
    rb = lax.broadcasted_iota(jnp.int32, (rows, RET_D), 0) // dt
    for g in range(bb // grp):
        rsl = slice(g * rows, (g + 1) * rows)
        for h in range(RET_HEADS):
            cols = slice(h * RET_D, (h + 1) * RET_D)
            lg = LOG_G[h]
            decay = jnp.where(keep, jnp.exp(jnp.where(keep, diff, 0.0) * lg), 0.0)
            q = q_ref[rsl, cols]
            k = k_ref[rsl, cols]
            vb = v_ref[rsl, cols].astype(BF16)
            inner = _nt_dot(q.astype(BF16), k.astype(BF16)) * decay
            o = _dot(inner.astype(BF16), vb)
            qd = (q * jnp.exp((rn + 1.0) * lg)).astype(BF16)
            kd = k * jnp.exp((dt - 1.0 - rn) * lg)
            for j in range(grp):
                b = g * grp + j
                s0 = s0_ref[b, h]
                o = o + jnp.where(rb == j, _dot(qd, s0.astype(BF16)), 0.0)
                kdj = jnp.where(rb == j, kd, 0.0).astype(BF16)
                s1_ref[b, h] = math.exp(dt * lg) * s0 + _tn_dot(kdj, vb)
            y_ref[rsl, cols] = _group_norm_gate(o, gn_ref[:, cols], rg_ref[rsl, cols])


def _retention_sample(rq, rk, rv, rg, gn, state, dt):
    n = rq.shape[0]
    db = n // dt
    bb = 8
    blk = pl.BlockSpec((bb * dt, SEG_W), lambda i: (i, 0))
    st = pl.BlockSpec((bb, RET_HEADS, RET_D, RET_D), lambda i: (i, 0, 0, 0))
    return pl.pallas_call(
        functools.partial(_ret_sample_kernel, bb=bb, dt=dt),
        grid=(db // bb,),
        in_specs=[blk, blk, blk, blk, pl.BlockSpec((1, SEG_W), lambda i: (0, 0)), st],
        out_specs=[blk, st],
        out_shape=[jax.ShapeDtypeStruct((n, SEG_W), F32), jax.ShapeDtypeStruct(state.shape, F32)],
        compiler_params=pltpu.CompilerParams(dimension_semantics=("parallel",), vmem_limit_bytes=VMEM_LIMIT),
        name="retention_sample",
    )(rq, rk, rv, rg, gn, state)


def _tri(n):
    r = lax.broadcasted_iota(jnp.int32, (n, n), 0)
    c = lax.broadcasted_iota(jnp.int32, (n, n), 1)
    return jnp.where(r > c, 1.0, 0.0).astype(BF16)


def _softplus2(z, mask):
    neg_abs = lax.bitcast_convert_type(lax.bitcast_convert_type(z, jnp.uint32) | jnp.uint32(0x80000000), F32)
    sp = jnp.maximum(z, 0.0) + jnp.log2(1.0 + jnp.exp2(neg_abs))
    return sp if mask is None else jnp.where(mask, sp, 0.0)


def _sp_suffix(sp, tri):
    return sp + _dot(sp.astype(BF16), tri)


def _sb_weights(z, suffix, carry, mask):
    a = jnp.exp2(z - (suffix + carry))
    return a if mask is None else jnp.where(mask, a, 0.0)


def _sb_prompt_kernel(bias_ref, q_ref, k_ref, v_ref, sg_ref, o_ref, qz_ref, acc_ref, carry_ref, *, tq, tk):
    hp = pl.program_id(1)
    qi = pl.program_id(2)
    nd = tq // tk
    lane = lax.broadcasted_iota(jnp.int32, (tq, LANES), 1)
    q = q_ref[...]
    qz_ref[0] = jnp.where(lane < SB_DH, q, jnp.zeros_like(q))
    qz_ref[1] = jnp.where(lane >= SB_DH, q, jnp.zeros_like(q))
    acc_ref[...] = jnp.zeros_like(acc_ref)
    carry_ref[...] = jnp.zeros_like(carry_ref)
    tri = _tri(tk)
    bias = [bias_ref[2 * hp] * LOG2E, bias_ref[2 * hp + 1] * LOG2E]
    q0 = qi * tq

    def block(k0, r0, mask):
        kt = k_ref[0, :, pl.ds(k0, tk)]
        vb = v_ref[pl.ds(k0, tk), :]
        for h in range(2):
            z = _dot(qz_ref[h, r0:, :], kt) + bias[h]
            suffix = _sp_suffix(_softplus2(z, mask), tri)
            carry = carry_ref[h, r0:, :]
            a = _sb_weights(z, suffix, carry, mask)
            carry_ref[h, r0:, :] = carry + suffix[:, 0:1]
            acc_ref[h, r0:, :] += _dot(a.astype(BF16), vb)

    for d in reversed(range(nd)):
        rl = lax.broadcasted_iota(jnp.int32, (tq - d * tk, tk), 0)
        cl = lax.broadcasted_iota(jnp.int32, (tq - d * tk, tk), 1)
        block(pl.multiple_of(q0 + d * tk, tk), d * tk, cl < rl)

    def body(i, _):
        block(pl.multiple_of(q0 - (i + 1) * tk, tk), 0, None)
        return 0

    lax.fori_loop(0, qi * nd, body, 0)
    o = jnp.where(lane < SB_DH, acc_ref[0], acc_ref[1])
    o_ref[...] = o * _silu(sg_ref[...])


def _sb_prompt(sb_bias, sqb, skb, svb, sg, batch, seq, tq, tk):
    nq = seq // tq
    qblk = pl.BlockSpec((tq, LANES), lambda b, hp, i: (b * nq + i, hp))
    ktblk = pl.BlockSpec((1, LANES, seq), lambda b, hp, i: (b * (SB_HEADS // 2) + hp, 0, 0))
    vblk = pl.BlockSpec((seq, LANES), lambda b, hp, i: (b, hp))
    return pl.pallas_call(
        functools.partial(_sb_prompt_kernel, tq=tq, tk=tk),
        grid=(batch, SB_HEADS // 2, nq),
        in_specs=[pl.BlockSpec(memory_space=pltpu.SMEM), qblk, ktblk, vblk, qblk],
        out_specs=qblk,
        out_shape=jax.ShapeDtypeStruct((batch * seq, SEG_W), F32),
        scratch_shapes=[pltpu.VMEM((2, tq, LANES), BF16), pltpu.VMEM((2, tq, LANES), F32),
                        pltpu.VMEM((2, tq, 1), F32)],
        compiler_params=pltpu.CompilerParams(dimension_semantics=("parallel", "parallel", "arbitrary"),
                                             vmem_limit_bytes=VMEM_LIMIT),
        name="sb_prompt",
    )(sb_bias, sqb, skb, svb, sg)


def _sb_sample_kernel(pt_ref, bias_ref, q_ref, kn_ref, vn_ref, sg_ref, *rest, pp, page, dt):
    k_refs = rest[:pp]
    v_refs = rest[pp:2 * pp]
    o_ref, qbd_ref, acc_ref, carry_ref = rest[2 * pp:]
    j = pl.program_id(1)
    rows = dt * SB_HEADS
    bias = bias_ref[...]

    def pages(kts, vts, mask):
        tri = _tri(kts[0].shape[1])
        qbd = qbd_ref[...]
        zs = [_dot(qbd, kt) + bias for kt in kts]
        suffixes = [_sp_suffix(_softplus2(z, mask), tri) for z in zs]
        carry = carry_ref[...]
        acc = acc_ref[...]
        for z, suffix, vt in zip(zs, suffixes, vts):
            a = _sb_weights(z, suffix, carry, mask)
            acc = acc + _nt_dot(a.astype(BF16), vt)
            carry = carry + suffix[:, 0:1]
        carry_ref[...] = carry
        acc_ref[...] = acc

    def pair(refs, i):
        return jnp.concatenate([refs[i - 1][0].astype(BF16), refs[i][0].astype(BF16)], axis=1)

    @pl.when(j == 0)
    def _start():
        r = lax.broadcasted_iota(jnp.int32, (rows, SEG_W), 0)
        c = lax.broadcasted_iota(jnp.int32, (rows, SEG_W), 1)
        q = q_ref[0]
        qbd_ref[...] = jnp.where(c // SB_DH == r % SB_HEADS, q, jnp.zeros_like(q))
        acc_ref[...] = jnp.zeros_like(acc_ref)
        carry_ref[...] = jnp.zeros_like(carry_ref)
        rq = lax.broadcasted_iota(jnp.int32, (rows, page), 0) // SB_HEADS
        cm = lax.broadcasted_iota(jnp.int32, (rows, page), 1)
        pages([kn_ref[0]], [vn_ref[0]], cm < rq)

    order = list(reversed(range(1, pp, 2)))
    pages([pair(k_refs, i) for i in order], [pair(v_refs, i) for i in order], None)

    @pl.when(j == pl.num_programs(1) - 1)
    def _fin():
        r = lax.broadcasted_iota(jnp.int32, (rows, SEG_W), 0)
        c = lax.broadcasted_iota(jnp.int32, (rows, SEG_W), 1)
        own = jnp.where(c // SB_DH == r % SB_HEADS, acc_ref[...], 0.0)
        o = jnp.sum(own.reshape(dt, SB_HEADS, SEG_W), axis=1)
        o_ref[0] = o * _silu(sg_ref[0])


def _sb_sample(page_table, bias_col, q_rep, k_new, v_new, sg, cache_k, cache_v, pp):
    db, n_pages = page_table.shape
    page = cache_k.shape[2]
    dt = sg.shape[1]
    rows = dt * SB_HEADS
    nstep = n_pages // pp
    per_b = lambda blk: pl.BlockSpec((1,) + blk, lambda b, j, pt: (b, 0, 0))

    def page_spec(i):
        return pl.BlockSpec((1, SEG_W, page), lambda b, j, pt: (pt[b, (nstep - 1 - j) * pp + i], 0, 0))

    grid_spec = pltpu.PrefetchScalarGridSpec(
        num_scalar_prefetch=1,
        grid=(db, nstep),
        in_specs=[pl.BlockSpec((rows, 1), lambda b, j, pt: (0, 0)),
                  per_b((rows, SEG_W)), per_b((SEG_W, page)), per_b((SEG_W, page)), per_b((dt, SEG_W))]
                 + [page_spec(i) for i in range(pp)] * 2,
        out_specs=per_b((dt, SEG_W)),
        scratch_shapes=[pltpu.VMEM((rows, SEG_W), BF16), pltpu.VMEM((rows, SEG_W), F32),
                        pltpu.VMEM((rows, 1), F32)],
    )
    return pl.pallas_call(
        functools.partial(_sb_sample_kernel, pp=pp, page=page, dt=dt),
        grid_spec=grid_spec,
        out_shape=jax.ShapeDtypeStruct((db, dt, SEG_W), F32),
        compiler_params=pltpu.CompilerParams(dimension_semantics=("parallel", "arbitrary"),
                                             vmem_limit_bytes=VMEM_LIMIT),
        name="sb_sample",
    )(page_table, bias_col, q_rep, k_new, v_new, sg, *([cache_k] * pp), *([cache_v] * pp))


def _out_kernel(yr_ref, ys_ref, h_ref, p_ref, wo_ref, wp_ref, wg_ref, gp_ref, gf_ref, o_ref):
    y = _dot(yr_ref[...].astype(BF16), wo_ref[:SEG_W, :]) + _dot(ys_ref[...].astype(BF16), wo_ref[SEG_W:, :])
    h1 = h_ref[...] + y
    e = _dot(p_ref[...].astype(BF16), wp_ref[...])
    e = (e * lax.rsqrt(jnp.mean(e * e, axis=-1, keepdims=True) + EPS)) * gp_ref[...]
    gate = 1.0 / (1.0 + jnp.exp(-_dot(h1.astype(BF16), wg_ref[...])))
    h2 = h1 + e * gate
    o_ref[...] = (h2 * lax.rsqrt(jnp.mean(h2 * h2, axis=-1, keepdims=True) + EPS)) * gf_ref[...]


def _out_proj(yr, ys, h, p, wo_bf, wp_bf, wg_bf, g_ple, g_final, tm):
    n, d = h.shape
    d_ple = p.shape[1]
    row = lambda i: (i, 0)
    const = lambda i: (0, 0)
    return pl.pallas_call(
        _out_kernel,
        grid=(n // tm,),
        in_specs=[pl.BlockSpec((tm, SEG_W), row), pl.BlockSpec((tm, SEG_W), row), pl.BlockSpec((tm, d), row),
                  pl.BlockSpec((tm, d_ple), row), pl.BlockSpec((2 * SEG_W, d), const),
                  pl.BlockSpec((d_ple, d), const), pl.BlockSpec((d, d), const),
                  pl.BlockSpec((1, d), const), pl.BlockSpec((1, d), const)],
        out_specs=pl.BlockSpec((tm, d), row),
        out_shape=jax.ShapeDtypeStruct((n, d), F32),
        compiler_params=pltpu.CompilerParams(dimension_semantics=("parallel",), vmem_limit_bytes=VMEM_LIMIT),
        name="out_proj_ple_norm",
    )(yr, ys, h, p, wo_bf, wp_bf, wg_bf, g_ple, g_final)


def kernel(x_prompt, x_sample, cache_sb_k, cache_sb_v, state_ret, page_table, p_prompt, p_sample,
           g_norm, w_in, sb_bias, g_ret_gn, w_out, w_ple, g_ple, w_ple_gate, g_final):
    b, t, d = x_prompt.shape
    db, dt, _ = x_sample.shape
    depth = w_in.shape[0]
    assert depth == 1, "single-layer trunk"
    n_pool, page = cache_sb_k.shape[1], cache_sb_k.shape[2]
    past = page_table.shape[1] * page
    tm_p = min(256, t)
    tm_s = min(256, db * dt)

    half = RET_D // 2
    inv = ROPE_BASE ** (-jnp.arange(half, dtype=F32) / half)
    inv128 = jnp.concatenate([inv, inv])[None, :]
    cos_p, sin_p = _rope_table(inv128, t, 0, t)
    cos_s, sin_s = _rope_table(inv128, tm_s, past, dt)

    w_bf = w_in[0].astype(BF16)
    wkv_t = jnp.swapaxes(w_bf[:, 5 * SEG_W:7 * SEG_W].reshape(d, 2, SEG_W), 0, 1).swapaxes(1, 2)
    gn = g_norm[0][None, :]
    ggn = g_ret_gn[0][None, :]
    wo_bf = w_out[0].astype(BF16)
    wp_bf = w_ple[0].astype(BF16)
    wg_bf = w_ple_gate[0].astype(BF16)
    gp = g_ple[0][None, :]
    gf = g_final[None, :]

    xp = x_prompt.reshape(b * t, d)
    rq, rk, rv, rg, sqb, sk_t, sv_t, sg, skb_t, svb = _project(xp, gn, w_bf, wkv_t, cos_p, sin_p, t // tm_p, tm_p, seq=t)
    yr, s_fin = _retention_prompt(rq, rk, rv, rg, ggn, b, t)
    tq = min(2048, t)
    ys = _sb_prompt(sb_bias[0], sqb, skb_t.reshape(b * (SB_HEADS // 2), LANES, t), svb, sg, b, t, tq, min(256, tq))
    y_prompt = _out_proj(yr.reshape(b * t, SEG_W), ys, xp, p_prompt[0].reshape(b * t, -1),
                         wo_bf, wp_bf, wg_bf, gp, gf, min(512, b * t))

    xs = x_sample.reshape(db * dt, d)
    rq_s, rk_s, rv_s, rg_s, sqb_s, sk_s, sv_s, sg_s, skb_s, svb_s = _project(xs, gn, w_bf, wkv_t, cos_s, sin_s, 1, tm_s)
    yr_s, s_new = _retention_sample(rq_s, rk_s, rv_s, rg_s, ggn, state_ret[0], dt)
    q_rep = jnp.repeat(sqb_s.reshape(db, dt, SEG_W), SB_HEADS, axis=1)
    pad = ((0, 0), (0, 0), (0, page - dt))
    k_new = jnp.pad(jnp.swapaxes(skb_s.reshape(db, dt, SEG_W), 1, 2), pad)
    v_new = jnp.pad(jnp.swapaxes(svb_s.reshape(db, dt, SEG_W), 1, 2), pad)
    bias_col = jnp.tile(sb_bias[0] * LOG2E, dt)[:, None]
    cache_kt = jnp.transpose(cache_sb_k[0], (0, 2, 3, 1)).reshape(n_pool, SEG_W, page)
    cache_vt = jnp.transpose(cache_sb_v[0], (0, 2, 3, 1)).reshape(n_pool, SEG_W, page)
    ys_s = _sb_sample(page_table, bias_col, q_rep, k_new, v_new, sg_s.reshape(db, dt, SEG_W),
                      cache_kt, cache_vt, 16)
    y_sample = _out_proj(yr_s, ys_s.reshape(db * dt, SEG_W), xs, p_sample[0].reshape(db * dt, -1),
                         wo_bf, wp_bf, wg_bf, gp, gf, min(512, db * dt))

    return (y_prompt.reshape(b, t, d),
            y_sample.reshape(db, dt, d),
            jnp.transpose(sk_t.reshape(1, b, SB_HEADS, SB_DH, t), (0, 1, 4, 2, 3)),
            jnp.transpose(sv_t.reshape(1, b, SB_HEADS, SB_DH, t), (0, 1, 4, 2, 3)),
            sk_s.reshape(1, db, dt, SB_HEADS, SB_DH),
            sv_s.reshape(1, db, dt, SB_HEADS, SB_DH),
            s_fin.reshape(1, b, RET_HEADS, RET_D, RET_D),
            s_new[None])
```

```python
import functools
import math

import numpy as np
import jax
import jax.numpy as jnp
from jax import lax
from jax.experimental import pallas as pl
from jax.experimental.pallas import tpu as pltpu

F32 = jnp.float32
BF16 = jnp.bfloat16

RET_HEADS = 4
RET_D = 128
SB_HEADS = 8
SB_DH = 64
SEG_W = 512
CHUNK = 128
ROPE_BASE = 10000.0
EPS = 1e-6
LOG2E = math.log2(math.e)
LANES = 128
VMEM_LIMIT = 48 * 1024 * 1024
FUSED_VMEM_LIMIT = 56 * 1024 * 1024

LOG_G = [float(np.log(np.float32(1.0) - np.float32(2.0) ** np.float32(-5.0 - h))) for h in range(RET_HEADS)]


def _nt_dot(a, b):
    return lax.dot_general(a, b, (((1,), (1,)), ((), ())), preferred_element_type=F32)


def _tn_dot(a, b):
    return lax.dot_general(a, b, (((0,), (0,)), ((), ())), preferred_element_type=F32)


def _dot(a, b):
    return jnp.dot(a, b, preferred_element_type=F32)


def _silu(x):
    return x / (1.0 + jnp.exp(-x))


def _rope_kernel(inv_ref, cos_ref, sin_ref, *, base, period, rows):
    r = lax.broadcasted_iota(jnp.int32, (rows, LANES), 0) + pl.program_id(0) * rows
    lane = lax.broadcasted_iota(jnp.int32, (rows, LANES), 1)
    pos = (base + r % period).astype(F32)
    ang = pos * inv_ref[...]
    cos_ref[...] = jnp.cos(ang)
    sin_ref[...] = jnp.where(lane < LANES // 2, -1.0, 1.0) * jnp.sin(ang)


def _rope_table(inv128, n_rows, base, period):
    rows = min(n_rows, 512)
    return pl.pallas_call(
        functools.partial(_rope_kernel, base=base, period=period, rows=rows),
        grid=(n_rows // rows,),
        in_specs=[pl.BlockSpec((1, LANES), lambda i: (0, 0))],
        out_specs=[pl.BlockSpec((rows, LANES), lambda i: (i, 0))] * 2,
        out_shape=[jax.ShapeDtypeStruct((n_rows, LANES), F32)] * 2,
        name="rope_table",
    )(inv128)


def _proj_kernel(x_ref, g_ref, w_ref, wkv_t_ref, cos_ref, sin_ref,
                 rq_ref, rk_ref, rv_ref, rg_ref, sqb_ref, sk_ref, sv_ref, sg_ref, skb_ref, svb_ref, *, kv_transposed):
    x = x_ref[...]
    ms = jnp.mean(x * x, axis=-1, keepdims=True)
    xn = ((x * lax.rsqrt(ms + EPS)) * g_ref[...]).astype(BF16)
    cos = cos_ref[...]
    sin = sin_ref[...]

    def seg(j):
        return _dot(xn, w_ref[:, j * SEG_W:(j + 1) * SEG_W])

    def rotary_store(p, out_ref, scale):
        for h in range(RET_HEADS):
            ph = p[:, h * RET_D:(h + 1) * RET_D]
            rot = ph * cos + pltpu.roll(ph, RET_D // 2, axis=1) * sin
            out_ref[:, h * RET_D:(h + 1) * RET_D] = rot if scale is None else rot * scale

    rotary_store(seg(0), rq_ref, None)
    rotary_store(seg(1), rk_ref, RET_D ** -0.5)
    rv_ref[...] = seg(2)
    rg_ref[...] = seg(3)
    sqb_ref[...] = (seg(4) * (LOG2E * SB_DH ** -0.5)).astype(BF16)
    if kv_transposed:
        sk_t = _nt_dot(wkv_t_ref[0], xn)
        sk_ref[0] = sk_t
        skb_ref[0] = sk_t.astype(BF16)
        sv_ref[0] = _nt_dot(wkv_t_ref[1], xn)
        svb_ref[...] = seg(6).astype(BF16)
    else:
        sk = seg(5)
        sk_ref[...] = sk
        skb_ref[...] = sk.astype(BF16)
        sv = seg(6)
        sv_ref[...] = sv
        svb_ref[...] = sv.astype(BF16)
    sg_ref[...] = seg(7)


def _project(x2d, g_norm, w_bf, wkv_t, cos_t, sin_t, table_period_blocks, tm, seq=None):
    n, d = x2d.shape
    d_in = w_bf.shape[1]
    row = lambda i: (i, 0)
    const2 = lambda i: (0, 0)
    f32_out = jax.ShapeDtypeStruct((n, SEG_W), F32)
    bf_out = jax.ShapeDtypeStruct((n, SEG_W), BF16)
    seg_spec = pl.BlockSpec((tm, SEG_W), row)
    tab_spec = pl.BlockSpec((tm, LANES), lambda i: (i % table_period_blocks, 0))
    if seq is None:
        kv_f32, kv_bf, kv_spec = f32_out, bf_out, seg_spec
    else:
        per_b = seq // tm
        kv_f32 = jax.ShapeDtypeStruct((n // seq, SEG_W, seq), F32)
        kv_bf = jax.ShapeDtypeStruct((n // seq, SEG_W, seq), BF16)
        kv_spec = pl.BlockSpec((1, SEG_W, tm), lambda i: (i // per_b, 0, i % per_b))
    return pl.pallas_call(
        functools.partial(_proj_kernel, kv_transposed=seq is not None),
        grid=(n // tm,),
        in_specs=[pl.BlockSpec((tm, d), row),
                  pl.BlockSpec((1, d), const2),
                  pl.BlockSpec((d, d_in), const2),
                  pl.BlockSpec((2, SEG_W, d), lambda i: (0, 0, 0)),
                  tab_spec, tab_spec],
        out_specs=[seg_spec, seg_spec, seg_spec, seg_spec, seg_spec, kv_spec, kv_spec, seg_spec, kv_spec, seg_spec],
        out_shape=[f32_out, f32_out, f32_out, f32_out, bf_out, kv_f32, kv_f32, f32_out, kv_bf, bf_out],
        compiler_params=pltpu.CompilerParams(dimension_semantics=("parallel",), vmem_limit_bytes=VMEM_LIMIT),
        name="rmsnorm_in_proj",
    )(x2d, g_norm, w_bf, wkv_t, cos_t, sin_t)


def _group_norm_gate(o, gn, rg):
    mu = jnp.mean(o, axis=-1, keepdims=True)
    d = o - mu
    var = jnp.mean(d * d, axis=-1, keepdims=True)
    return (d * lax.rsqrt(var + EPS)) * gn * _silu(rg)


def _ret_prompt_kernel(q_ref, k_ref, v_ref, rg_ref, gn_ref, y_ref, sfin_ref,
                       state_ref, decay_ref, qdec_ref, kdec_ref, *, batch, chunk):
    c = pl.program_id(0)

    @pl.when(c == 0)
    def _init():
        state_ref[...] = jnp.zeros_like(state_ref)
        n = lax.broadcasted_iota(jnp.int32, (chunk, chunk), 0).astype(F32)
        m = lax.broadcasted_iota(jnp.int32, (chunk, chunk), 1).astype(F32)
        diff = n - m
        causal = diff >= 0
        for h in range(RET_HEADS):
            decay_ref[h] = jnp.where(causal, jnp.exp(jnp.where(causal, diff, 0.0) * LOG_G[h]), 0.0)
            qdec_ref[h] = jnp.exp((n + 1.0) * LOG_G[h])
            kdec_ref[h] = jnp.exp((chunk - 1.0 - n) * LOG_G[h])

    for b in range(batch):
        for h in range(RET_HEADS):
            cols = slice(h * RET_D, (h + 1) * RET_D)
            q = q_ref[b, :, cols]
            k = k_ref[b, :, cols]
            vb = v_ref[b, :, cols].astype(BF16)
            s = state_ref[b * RET_HEADS + h]
            inner = _nt_dot(q.astype(BF16), k.astype(BF16)) * decay_ref[h]
            o = _dot(inner.astype(BF16), vb) + _dot((q * qdec_ref[h]).astype(BF16), s.astype(BF16))
            state_ref[b * RET_HEADS + h] = (math.exp(chunk * LOG_G[h]) * s
                                            + _tn_dot((k * kdec_ref[h]).astype(BF16), vb))
            y_ref[b, :, cols] = _group_norm_gate(o, gn_ref[:, cols], rg_ref[b, :, cols])

    @pl.when(c == pl.num_programs(0) - 1)
    def _fin():
        sfin_ref[...] = state_ref[...]


def _retention_prompt(rq, rk, rv, rg, gn, batch, seq):
    nc = seq // CHUNK
    blk = pl.BlockSpec((batch, CHUNK, SEG_W), lambda c: (0, c, 0))
    sq_mat = pltpu.VMEM((RET_HEADS, CHUNK, CHUNK), F32)
    return pl.pallas_call(
        functools.partial(_ret_prompt_kernel, batch=batch, chunk=CHUNK),
        grid=(nc,),
        in_specs=[blk, blk, blk, blk, pl.BlockSpec((1, SEG_W), lambda c: (0, 0))],
        out_specs=[blk, pl.BlockSpec((batch * RET_HEADS, RET_D, RET_D), lambda c: (0, 0, 0))],
        out_shape=[jax.ShapeDtypeStruct((batch, seq, SEG_W), F32),
                   jax.ShapeDtypeStruct((batch * RET_HEADS, RET_D, RET_D), F32)],
        scratch_shapes=[pltpu.VMEM((batch * RET_HEADS, RET_D, RET_D), F32), sq_mat, sq_mat, sq_mat],
        compiler_params=pltpu.CompilerParams(dimension_semantics=("arbitrary",), vmem_limit_bytes=VMEM_LIMIT),
        name="retention_prompt",
    )(rq.reshape(batch, seq, SEG_W), rk.reshape(batch, seq, SEG_W), rv.reshape(batch, seq, SEG_W),
      rg.reshape(batch, seq, SEG_W), gn)


def _ret_sample_kernel(q_ref, k_ref, v_ref, rg_ref, gn_ref, s0_ref, y_ref, s1_ref, *, bb, dt):
    grp = 16 // dt
    rows = grp * dt
    ri = lax.broadcasted_iota(jnp.int32, (rows, rows), 0)
    ci = lax.broadcasted_iota(jnp.int32, (rows, rows), 1)
    diff = (ri % dt - ci % dt).astype(F32)
    keep = (ri // dt == ci // dt) & (diff >= 0)
    rn = (lax.broadcasted_iota(jnp.int32, (rows, RET_D), 0) % dt).astype(F32)
    rb = lax.broadcasted_iota(jnp.int32, (rows, RET_D), 0) // dt
    for g in range(bb // grp):
        rsl = slice(g * rows, (g + 1) * rows)
        for h in range(RET_HEADS):
            cols = slice(h * RET_D, (h + 1) * RET_D)
            lg = LOG_G[h]
            decay = jnp.where(keep, jnp.exp(jnp.where(keep, diff, 0.0) * lg), 0.0)
            q = q_ref[rsl, cols]
            k = k_ref[rsl, cols]
            vb = v_ref[rsl, cols].astype(BF16)
            inner = _nt_dot(q.astype(BF16), k.astype(BF16)) * decay
            o = _dot(inner.astype(BF16), vb)
            qd = (q * jnp.exp((rn + 1.0) * lg)).astype(BF16)
            kd = k * jnp.exp((dt - 1.0 - rn) * lg)
            for j in range(grp):
                b = g * grp + j
                s0 = s0_ref[b, h]
                o = o + jnp.where(rb == j, _dot(qd, s0.astype(BF16)), 0.0)
                kdj = jnp.where(rb == j, kd, 0.0).astype(BF16)
                s1_ref[b, h] = math.exp(dt * lg) * s0 + _tn_dot(kdj, vb)
            y_ref[rsl, cols] = _group_norm_gate(o, gn_ref[:, cols], rg_ref[rsl, cols])


def _retention_sample(rq, rk, rv, rg, gn, state, dt):
    n = rq.shape[0]
    db = n // dt
    bb = 8
    blk = pl.BlockSpec((bb * dt, SEG_W), lambda i: (i, 0))
    st = pl.BlockSpec((bb, RET_HEADS, RET_D, RET_D), lambda i: (i, 0, 0, 0))
    return pl.pallas_call(
        functools.partial(_ret_sample_kernel, bb=bb, dt=dt),
        grid=(db // bb,),
        in_specs=[blk, blk, blk, blk, pl.BlockSpec((1, SEG_W), lambda i: (0, 0)), st],
        out_specs=[blk, st],
        out_shape=[jax.ShapeDtypeStruct((n, SEG_W), F32), jax.ShapeDtypeStruct(state.shape, F32)],
        compiler_params=pltpu.CompilerParams(dimension_semantics=("parallel",), vmem_limit_bytes=VMEM_LIMIT),
        name="retention_sample",
    )(rq, rk, rv, rg, gn, state)


def _tri(n):
    r = lax.broadcasted_iota(jnp.int32, (n, n), 0)
    c = lax.broadcasted_iota(jnp.int32, (n, n), 1)
    return jnp.where(r > c, 1.0, 0.0).astype(BF16)


def _softplus2(z, mask):
    neg_abs = lax.bitcast_convert_type(lax.bitcast_convert_type(z, jnp.uint32) | jnp.uint32(0x80000000), F32)
    sp = jnp.maximum(z, 0.0) + jnp.log2(1.0 + jnp.exp2(neg_abs))
    return sp if mask is None else jnp.where(mask, sp, 0.0)


def _sp_suffix(sp, tri):
    return sp + _dot(sp.astype(BF16), tri)


def _sb_weights(z, suffix, carry, mask):
    a = jnp.exp2(z - (suffix + carry))
    return a if mask is None else jnp.where(mask, a, 0.0)


def _head_rows_mask(rows):
    r = lax.broadcasted_iota(jnp.int32, (rows, SEG_W), 0)
    c = lax.broadcasted_iota(jnp.int32, (rows, SEG_W), 1)
    return c // SB_DH == r % SB_HEADS


def _span_logits(qbd, bias, kts):
    return [_dot(qbd, kt) + bias for kt in kts]


def _span_suffixes(zs, mask):
    tri = _tri(zs[0].shape[1])
    return [_sp_suffix(_softplus2(z, mask), tri) for z in zs]


def _span_accumulate(zs, suffixes, vts, carry, acc, mask):
    for z, suffix, vt in zip(zs, suffixes, vts):
        a = _sb_weights(z, suffix, carry, mask)
        acc = acc + _nt_dot(a.astype(BF16), vt)
        carry = carry + suffix[:, 0:1]
    return carry, acc


def _diag_pair_split(pairs, nd):
    total = nd * (nd + 1) // 2
    base = [pairs * (nd - d) // total for d in range(nd)]
    by_remainder = sorted(range(nd), key=lambda d: -((pairs * (nd - d)) % total))
    for d in by_remainder[:pairs - sum(base)]:
        base[d] += 1
    return base


def _sb_fused_kernel(pt_ref, bias_ref, q_ref, k_ref, v_ref, sg_ref, qs_ref, bcol_ref, ck_ref, cv_ref,
                     o_ref, os_ref,
                     qz_ref, acc_ref, carry_ref, kbuf, vbuf, ksem, vsem, qbd_ref, sacc_ref, scarry_ref,
                     *, tq, tk, n_pages, ppi, diag_pairs):
    hp = pl.program_id(1)
    qi = pl.program_id(2)
    nd = tq // tk
    rows = qbd_ref.shape[0]
    dt = rows // SB_HEADS
    pairs = n_pages // 2
    ipe = pairs // ppi
    group = pl.program_id(0) * pl.num_programs(1) + hp
    epg = os_ref.shape[0]
    first = qi * qi

    def page_copies(le, pair, slot, j):
        cps = []
        for half in range(2):
            pg = pt_ref[group * epg + le, n_pages - 2 - 2 * pair + half]
            cps.append(pltpu.make_async_copy(ck_ref.at[pg], kbuf.at[slot, 2 * j + half], ksem.at[slot]))
            cps.append(pltpu.make_async_copy(cv_ref.at[pg], vbuf.at[slot, 2 * j + half], vsem.at[slot]))
        return cps

    def fetch(le, pair0, n, slot):
        for j in range(n):
            for cp in page_copies(le, pair0 + j, slot, j):
                cp.start()

    def arrive(le, pair0, n, slot):
        for j in range(n):
            for cp in page_copies(le, pair0 + j, slot, j):
                cp.wait()

    def seq_begin(le):
        q = qs_ref[le]
        qbd_ref[...] = jnp.where(_head_rows_mask(rows), q, jnp.zeros_like(q))
        sacc_ref[...] = jnp.zeros_like(sacc_ref)
        scarry_ref[...] = jnp.zeros_like(scarry_ref)

    def span(buf, slot, j):
        return jnp.concatenate([buf[slot, 2 * j].astype(BF16), buf[slot, 2 * j + 1].astype(BF16)], axis=1)

    def seq_end(le):
        own = jnp.where(_head_rows_mask(rows), sacc_ref[...], 0.0)
        os_ref[le] = jnp.sum(own.reshape(dt, SB_HEADS, SEG_W), axis=1)

    lane = lax.broadcasted_iota(jnp.int32, (tq, LANES), 1)
    q = q_ref[...]
    qz_ref[0] = jnp.where(lane < SB_DH, q, jnp.zeros_like(q))
    qz_ref[1] = jnp.where(lane >= SB_DH, q, jnp.zeros_like(q))
    acc_ref[...] = jnp.zeros_like(acc_ref)
    carry_ref[...] = jnp.zeros_like(carry_ref)
    tri = _tri(tk)
    bias = [bias_ref[2 * hp] * LOG2E, bias_ref[2 * hp + 1] * LOG2E]
    q0 = qi * tq

    def block(k0, r0, mask, slot, n):
        zs_s = _span_logits(qbd_ref[...], bcol_ref[...], [span(kbuf, slot, j) for j in range(n)])
        scarry, sacc = _span_accumulate(zs_s, _span_suffixes(zs_s, None), [span(vbuf, slot, j) for j in range(n)],
                                        scarry_ref[...], sacc_ref[...], None)
        scarry_ref[...] = scarry
        sacc_ref[...] = sacc
        kt = k_ref[0, :, pl.ds(k0, tk)]
        vb = v_ref[pl.ds(k0, tk), :]
        for h in range(2):
            z = _dot(qz_ref[h, r0:, :], kt) + bias[h]
            suffix = _sp_suffix(_softplus2(z, mask), tri)
            carry = carry_ref[h, r0:, :]
            a = _sb_weights(z, suffix, carry, mask)
            carry_ref[h, r0:, :] = carry + suffix[:, 0:1]
            acc_ref[h, r0:, :] += _dot(a.astype(BF16), vb)

    seq_begin(first)
    fetch(first, 0, diag_pairs[nd - 1], 0)
    done = 0
    for k, d in enumerate(reversed(range(nd))):
        n, slot = diag_pairs[d], k % 2
        if d > 0:
            fetch(first, done + n, diag_pairs[d - 1], 1 - slot)
        else:
            @pl.when(qi > 0)
            def _():
                fetch(first + 1, 0, ppi, 1 - slot)
        arrive(first, done, n, slot)
        rl = lax.broadcasted_iota(jnp.int32, (tq - d * tk, tk), 0)
        cl = lax.broadcasted_iota(jnp.int32, (tq - d * tk, tk), 1)
        block(pl.multiple_of(q0 + d * tk, tk), d * tk, cl < rl, slot, n)
        done += n
    seq_end(first)

    def body(i, _):
        slot = (nd + i) % 2
        le = first + 1 + i // ipe
        part = i % ipe

        @pl.when(part == 0)
        def _():
            seq_begin(le)

        @pl.when(i + 1 < qi * nd)
        def _():
            fetch(first + 1 + (i + 1) // ipe, ((i + 1) % ipe) * ppi, ppi, 1 - slot)

        arrive(le, part * ppi, ppi, slot)
        block(pl.multiple_of(q0 - (i + 1) * tk, tk), 0, None, slot, ppi)

        @pl.when(part == ipe - 1)
        def _():
            seq_end(le)
        return 0

    lax.fori_loop(0, qi * nd, body, 0)
    o = jnp.where(lane < SB_DH, acc_ref[0], acc_ref[1])
    o_ref[...] = o * _silu(sg_ref[...])


def _sb_fused(page_table, sb_bias, sqb, skb_t, svb, sg, q_rep, bias_col, cache_kt, cache_vt, batch, seq, tq, tk):
    nq = seq // tq
    nd = tq // tk
    hpairs = SB_HEADS // 2
    db, n_pages = page_table.shape
    rows = q_rep.shape[1]
    page = cache_kt.shape[2]
    pairs = n_pages // 2
    ppi = 8
    epg = nq * nq
    assert n_pages % 2 == 0 and pairs % ppi == 0 and nd % (pairs // ppi) == 0 and nd // (pairs // ppi) == 2
    assert db == batch * hpairs * epg, "sample batch must tile over (batch, head pair, query block)"
    diag_pairs = _diag_pair_split(pairs, nd)
    qblk = pl.BlockSpec((tq, LANES), lambda b, hp, i, pt: (b * nq + i, hp))
    ktblk = pl.BlockSpec((1, LANES, seq), lambda b, hp, i, pt: (b * hpairs + hp, 0, 0), pipeline_mode=pl.Buffered(1))
    vblk = pl.BlockSpec((seq, LANES), lambda b, hp, i, pt: (b, hp), pipeline_mode=pl.Buffered(1))
    grp = lambda shape: pl.BlockSpec((epg,) + shape, lambda b, hp, i, pt: (b * hpairs + hp, 0, 0))
    grid_spec = pltpu.PrefetchScalarGridSpec(
        num_scalar_prefetch=1,
        grid=(batch, hpairs, nq),
        in_specs=[pl.BlockSpec(memory_space=pltpu.SMEM), qblk, ktblk, vblk, qblk,
                  grp((rows, SEG_W)), pl.BlockSpec((rows, 1), lambda b, hp, i, pt: (0, 0)),
                  pl.BlockSpec(memory_space=pl.ANY), pl.BlockSpec(memory_space=pl.ANY)],
        out_specs=[qblk, grp((rows // SB_HEADS, SEG_W))],
        scratch_shapes=[pltpu.VMEM((2, tq, LANES), BF16), pltpu.VMEM((2, tq, LANES), F32),
                        pltpu.VMEM((2, tq, 1), F32),
                        pltpu.VMEM((2, 2 * ppi, SEG_W, page), F32), pltpu.VMEM((2, 2 * ppi, SEG_W, page), F32),
                        pltpu.SemaphoreType.DMA((2,)), pltpu.SemaphoreType.DMA((2,)),
                        pltpu.VMEM((rows, SEG_W), BF16), pltpu.VMEM((rows, SEG_W), F32),
                        pltpu.VMEM((rows, 1), F32)],
    )
    return pl.pallas_call(
        functools.partial(_sb_fused_kernel, tq=tq, tk=tk, n_pages=n_pages, ppi=ppi, diag_pairs=diag_pairs),
        grid_spec=grid_spec,
        out_shape=[jax.ShapeDtypeStruct((batch * seq, SEG_W), F32),
                   jax.ShapeDtypeStruct((db, rows // SB_HEADS, SEG_W), F32)],
        compiler_params=pltpu.CompilerParams(dimension_semantics=("arbitrary", "arbitrary", "arbitrary"),
                                             vmem_limit_bytes=FUSED_VMEM_LIMIT),
        name="sb_fused",
    )(page_table, sb_bias, sqb, skb_t, svb, sg, q_rep, bias_col, cache_kt, cache_vt)


def _sb_sample_finish_kernel(bias_ref, q_ref, kn_ref, vn_ref, past_ref, sg_ref, o_ref, *, bb, page, dt):
    rows = dt * SB_HEADS
    own_rows = _head_rows_mask(rows)
    rq = lax.broadcasted_iota(jnp.int32, (rows, page), 0) // SB_HEADS
    cm = lax.broadcasted_iota(jnp.int32, (rows, page), 1)
    zero_carry = jnp.zeros((rows, 1), F32)
    for e in range(bb):
        q = q_ref[e]
        qbd = jnp.where(own_rows, q, jnp.zeros_like(q))
        zs = _span_logits(qbd, bias_ref[...], [kn_ref[e]])
        c0, acc = _span_accumulate(zs, _span_suffixes(zs, cm < rq), [vn_ref[e]], zero_carry,
                                   jnp.zeros((rows, SEG_W), F32), cm < rq)
        past = jnp.broadcast_to(past_ref[e][:, None, :], (dt, SB_HEADS, SEG_W)).reshape(rows, SEG_W)
        own = jnp.where(own_rows, acc + jnp.exp2(-c0) * past, 0.0)
        o = jnp.sum(own.reshape(dt, SB_HEADS, SEG_W), axis=1)
        o_ref[e] = o * _silu(sg_ref[e])


def _sb_sample_finish(bias_col, q_rep, k_new, v_new, past, sg):
    db, rows, _ = q_rep.shape
    page = k_new.shape[2]
    dt = sg.shape[1]
    bb = 8
    per_b = lambda blk: pl.BlockSpec((bb,) + blk, lambda i: (i, 0, 0))
    return pl.pallas_call(
        functools.partial(_sb_sample_finish_kernel, bb=bb, page=page, dt=dt),
        grid=(db // bb,),
        in_specs=[pl.BlockSpec((rows, 1), lambda i: (0, 0)),
                  per_b((rows, SEG_W)), per_b((SEG_W, page)), per_b((SEG_W, page)),
                  per_b((dt, SEG_W)), per_b((dt, SEG_W))],
        out_specs=per_b((dt, SEG_W)),
        out_shape=jax.ShapeDtypeStruct((db, dt, SEG_W), F32),
        compiler_params=pltpu.CompilerParams(dimension_semantics=("parallel",), vmem_limit_bytes=VMEM_LIMIT),
        name="sb_sample_finish",
    )(bias_col, q_rep, k_new, v_new, past, sg)


def _out_kernel(yr_ref, ys_ref, h_ref, p_ref, wo_ref, wp_ref, wg_ref, gp_ref, gf_ref, o_ref):
    y = _dot(yr_ref[...].astype(BF16), wo_ref[:SEG_W, :]) + _dot(ys_ref[...].astype(BF16), wo_ref[SEG_W:, :])
    h1 = h_ref[...] + y
    e = _dot(p_ref[...].astype(BF16), wp_ref[...])
    e = (e * lax.rsqrt(jnp.mean(e * e, axis=-1, keepdims=True) + EPS)) * gp_ref[...]
    gate = 1.0 / (1.0 + jnp.exp(-_dot(h1.astype(BF16), wg_ref[...])))
    h2 = h1 + e * gate
    o_ref[...] = (h2 * lax.rsqrt(jnp.mean(h2 * h2, axis=-1, keepdims=True) + EPS)) * gf_ref[...]


def _out_proj(yr, ys, h, p, wo_bf, wp_bf, wg_bf, g_ple, g_final, tm):
    n, d = h.shape
    d_ple = p.shape[1]
    row = lambda i: (i, 0)
    const = lambda i: (0, 0)
    return pl.pallas_call(
        _out_kernel,
        grid=(n // tm,),
        in_specs=[pl.BlockSpec((tm, SEG_W), row), pl.BlockSpec((tm, SEG_W), row), pl.BlockSpec((tm, d), row),
                  pl.BlockSpec((tm, d_ple), row), pl.BlockSpec((2 * SEG_W, d), const),
                  pl.BlockSpec((d_ple, d), const), pl.BlockSpec((d, d), const),
                  pl.BlockSpec((1, d), const), pl.BlockSpec((1, d), const)],
        out_specs=pl.BlockSpec((tm, d), row),
        out_shape=jax.ShapeDtypeStruct((n, d), F32),
        compiler_params=pltpu.CompilerParams(dimension_semantics=("parallel",), vmem_limit_bytes=VMEM_LIMIT),
        name="out_proj_ple_norm",
    )(yr, ys, h, p, wo_bf, wp_bf, wg_bf, g_ple, g_final)


def kernel(x_prompt, x_sample, cache_sb_k, cache_sb_v, state_ret, page_table, p_prompt, p_sample,
           g_norm, w_in, sb_bias, g_ret_gn, w_out, w_ple, g_ple, w_ple_gate, g_final):
    b, t, d = x_prompt.shape
    db, dt, _ = x_sample.shape
    depth = w_in.shape[0]
    assert depth == 1, "single-layer trunk"
    n_pool, page = cache_sb_k.shape[1], cache_sb_k.shape[2]
    past = page_table.shape[1] * page
    tm_p = min(256, t)
    tm_s = min(256, db * dt)

    half = RET_D // 2
    inv = ROPE_BASE ** (-jnp.arange(half, dtype=F32) / half)
    inv128 = jnp.concatenate([inv, inv])[None, :]
    cos_p, sin_p = _rope_table(inv128, t, 0, t)
    cos_s, sin_s = _rope_table(inv128, tm_s, past, dt)

    w_bf = w_in[0].astype(BF16)
    wkv_t = jnp.swapaxes(w_bf[:, 5 * SEG_W:7 * SEG_W].reshape(d, 2, SEG_W), 0, 1).swapaxes(1, 2)
    gn = g_norm[0][None, :]
    ggn = g_ret_gn[0][None, :]
    wo_bf = w_out[0].astype(BF16)
    wp_bf = w_ple[0].astype(BF16)
    wg_bf = w_ple_gate[0].astype(BF16)
    gp = g_ple[0][None, :]
    gf = g_final[None, :]

    xp = x_prompt.reshape(b * t, d)
    rq, rk, rv, rg, sqb, sk_t, sv_t, sg, skb_t, svb = _project(xp, gn, w_bf, wkv_t, cos_p, sin_p, t // tm_p, tm_p, seq=t)
    xs = x_sample.reshape(db * dt, d)
    rq_s, rk_s, rv_s, rg_s, sqb_s, sk_s, sv_s, sg_s, skb_s, svb_s = _project(xs, gn, w_bf, wkv_t, cos_s, sin_s, 1, tm_s)

    q_rep = jnp.repeat(sqb_s.reshape(db, dt, SEG_W), SB_HEADS, axis=1)
    bias_col = jnp.tile(sb_bias[0] * LOG2E, dt)[:, None]
    cache_kt = jnp.transpose(cache_sb_k[0], (0, 2, 3, 1)).reshape(n_pool, SEG_W, page)
    cache_vt = jnp.transpose(cache_sb_v[0], (0, 2, 3, 1)).reshape(n_pool, SEG_W, page)
    tq = min(2048, t)
    ys, past_s = _sb_fused(page_table, sb_bias[0], sqb, skb_t.reshape(b * (SB_HEADS // 2), LANES, t), svb, sg,
                           q_rep, bias_col, cache_kt, cache_vt, b, t, tq, min(256, tq))
    pad = ((0, 0), (0, 0), (0, page - dt))
    k_new = jnp.pad(jnp.swapaxes(skb_s.reshape(db, dt, SEG_W), 1, 2), pad)
    v_new = jnp.pad(jnp.swapaxes(svb_s.reshape(db, dt, SEG_W), 1, 2), pad)
    ys_s = _sb_sample_finish(bias_col, q_rep, k_new, v_new, past_s, sg_s.reshape(db, dt, SEG_W))

    yr, s_fin = _retention_prompt(rq, rk, rv, rg, ggn, b, t)
    yr_s, s_new = _retention_sample(rq_s, rk_s, rv_s, rg_s, ggn, state_ret[0], dt)
    y_prompt = _out_proj(yr.reshape(b * t, SEG_W), ys, xp, p_prompt[0].reshape(b * t, -1),
                         wo_bf, wp_bf, wg_bf, gp, gf, min(512, b * t))
    y_sample = _out_proj(yr_s, ys_s.reshape(db * dt, SEG_W), xs, p_sample[0].reshape(db * dt, -1),
                         wo_bf, wp_bf, wg_bf, gp, gf, min(512, db * dt))

    return (y_prompt.reshape(b, t, d),
            y_sample.reshape(db, dt, d),
            jnp.transpose(sk_t.reshape(1, b, SB_HEADS, SB_DH, t), (0, 1, 4, 2, 3)),
            jnp.transpose(sv_t.reshape(1, b, SB_HEADS, SB_DH, t), (0, 1, 4, 2, 3)),
            sk_s.reshape(1, db, dt, SB_HEADS, SB_DH),
            sv_s.reshape(1, db, dt, SB_HEADS, SB_DH),
            s_fin.reshape(1, b, RET_HEADS, RET_D, RET_D),
            s_new[None])
```

```python
import functools
import math

import numpy as np
import jax
import jax.numpy as jnp
from jax import lax
from jax.experimental import pallas as pl
from jax.experimental.pallas import tpu as pltpu

F32 = jnp.float32
BF16 = jnp.bfloat16

RET_HEADS = 4
RET_D = 128
SB_HEADS = 8
SB_DH = 64
SEG_W = 512
CHUNK = 128
ROPE_BASE = 10000.0
EPS = 1e-6
LOG2E = math.log2(math.e)
LANES = 128
VMEM_LIMIT = 48 * 1024 * 1024
FUSED_VMEM_LIMIT = 56 * 1024 * 1024

LOG_G = [float(np.log(np.float32(1.0) - np.float32(2.0) ** np.float32(-5.0 - h))) for h in range(RET_HEADS)]


def _nt_dot(a, b):
    return lax.dot_general(a, b, (((1,), (1,)), ((), ())), preferred_element_type=F32)


def _tn_dot(a, b):
    return lax.dot_general(a, b, (((0,), (0,)), ((), ())), preferred_element_type=F32)


def _dot(a, b):
    return jnp.dot(a, b, preferred_element_type=F32)


def _silu(x):
    return x / (1.0 + jnp.exp(-x))


def _rope_kernel(inv_ref, cos_ref, sin_ref, *, base, period, rows):
    r = lax.broadcasted_iota(jnp.int32, (rows, LANES), 0) + pl.program_id(0) * rows
    lane = lax.broadcasted_iota(jnp.int32, (rows, LANES), 1)
    pos = (base + r % period).astype(F32)
    ang = pos * inv_ref[...]
    cos_ref[...] = jnp.cos(ang)
    sin_ref[...] = jnp.where(lane < LANES // 2, -1.0, 1.0) * jnp.sin(ang)


def _rope_table(inv128, n_rows, base, period):
    rows = min(n_rows, 512)
    return pl.pallas_call(
        functools.partial(_rope_kernel, base=base, period=period, rows=rows),
        grid=(n_rows // rows,),
        in_specs=[pl.BlockSpec((1, LANES), lambda i: (0, 0))],
        out_specs=[pl.BlockSpec((rows, LANES), lambda i: (i, 0))] * 2,
        out_shape=[jax.ShapeDtypeStruct((n_rows, LANES), F32)] * 2,
        name="rope_table",
    )(inv128)


def _proj_kernel(x_ref, g_ref, w_ref, wkv_t_ref, cos_ref, sin_ref,
                 rq_ref, rk_ref, rv_ref, rg_ref, sqb_ref, sk_ref, sv_ref, sg_ref, skb_ref, svb_ref, *, kv_transposed):
    x = x_ref[...]
    ms = jnp.mean(x * x, axis=-1, keepdims=True)
    xn = ((x * lax.rsqrt(ms + EPS)) * g_ref[...]).astype(BF16)
    cos = cos_ref[...]
    sin = sin_ref[...]

    def seg(j):
        return _dot(xn, w_ref[:, j * SEG_W:(j + 1) * SEG_W])

    def rotary_store(p, out_ref, scale):
        for h in range(RET_HEADS):
            ph = p[:, h * RET_D:(h + 1) * RET_D]
            rot = ph * cos + pltpu.roll(ph, RET_D // 2, axis=1) * sin
            out_ref[:, h * RET_D:(h + 1) * RET_D] = rot if scale is None else rot * scale

    rotary_store(seg(0), rq_ref, None)
    rotary_store(seg(1), rk_ref, RET_D ** -0.5)
    rv_ref[...] = seg(2)
    rg_ref[...] = seg(3)
    sqb_ref[...] = (seg(4) * (LOG2E * SB_DH ** -0.5)).astype(BF16)
    if kv_transposed:
        sk_t = _nt_dot(wkv_t_ref[...], xn)
        sk_ref[0] = sk_t
        skb_ref[0] = sk_t.astype(BF16)
        sv = seg(6)
        sv_ref[0] = sv.T
        svb_ref[...] = sv.astype(BF16)
    else:
        sk = seg(5)
        sk_ref[...] = sk
        skb_ref[...] = sk.astype(BF16)
        sv = seg(6)
        sv_ref[...] = sv
        svb_ref[...] = sv.astype(BF16)
    sg_ref[...] = seg(7)


def _project(x2d, g_norm, w_bf, wkv_t, cos_t, sin_t, table_period_blocks, tm, seq=None):
    n, d = x2d.shape
    d_in = w_bf.shape[1]
    row = lambda i: (i, 0)
    const2 = lambda i: (0, 0)
    f32_out = jax.ShapeDtypeStruct((n, SEG_W), F32)
    bf_out = jax.ShapeDtypeStruct((n, SEG_W), BF16)
    seg_spec = pl.BlockSpec((tm, SEG_W), row)
    tab_spec = pl.BlockSpec((tm, LANES), lambda i: (i % table_period_blocks, 0))
    if seq is None:
        kv_f32, kv_bf, kv_spec = f32_out, bf_out, seg_spec
    else:
        per_b = seq // tm
        kv_f32 = jax.ShapeDtypeStruct((n // seq, SEG_W, seq), F32)
        kv_bf = jax.ShapeDtypeStruct((n // seq, SEG_W, seq), BF16)
        kv_spec = pl.BlockSpec((1, SEG_W, tm), lambda i: (i // per_b, 0, i % per_b))
    return pl.pallas_call(
        functools.partial(_proj_kernel, kv_transposed=seq is not None),
        grid=(n // tm,),
        in_specs=[pl.BlockSpec((tm, d), row),
                  pl.BlockSpec((1, d), const2),
                  pl.BlockSpec((d, d_in), const2),
                  pl.BlockSpec((SEG_W, d), const2),
                  tab_spec, tab_spec],
        out_specs=[seg_spec, seg_spec, seg_spec, seg_spec, seg_spec, kv_spec, kv_spec, seg_spec, kv_spec, seg_spec],
        out_shape=[f32_out, f32_out, f32_out, f32_out, bf_out, kv_f32, kv_f32, f32_out, kv_bf, bf_out],
        compiler_params=pltpu.CompilerParams(dimension_semantics=("parallel",), vmem_limit_bytes=VMEM_LIMIT),
        name="rmsnorm_in_proj",
    )(x2d, g_norm, w_bf, wkv_t, cos_t, sin_t)


def _group_norm_gate(o, gn, rg):
    mu = jnp.mean(o, axis=-1, keepdims=True)
    d = o - mu
    var = jnp.mean(d * d, axis=-1, keepdims=True)
    return (d * lax.rsqrt(var + EPS)) * gn * _silu(rg)


def _ret_prompt_kernel(q_ref, k_ref, v_ref, rg_ref, gn_ref, y_ref, sfin_ref,
                       state_ref, decay_ref, qdec_ref, kdec_ref, *, batch, chunk):
    c = pl.program_id(0)

    @pl.when(c == 0)
    def _init():
        state_ref[...] = jnp.zeros_like(state_ref)
        n = lax.broadcasted_iota(jnp.int32, (chunk, chunk), 0).astype(F32)
        m = lax.broadcasted_iota(jnp.int32, (chunk, chunk), 1).astype(F32)
        diff = n - m
        causal = diff >= 0
        for h in range(RET_HEADS):
            decay_ref[h] = jnp.where(causal, jnp.exp(jnp.where(causal, diff, 0.0) * LOG_G[h]), 0.0)
            qdec_ref[h] = jnp.exp((n + 1.0) * LOG_G[h])
            kdec_ref[h] = jnp.exp((chunk - 1.0 - n) * LOG_G[h])

    for b in range(batch):
        for h in range(RET_HEADS):
            cols = slice(h * RET_D, (h + 1) * RET_D)
            q = q_ref[b, :, cols]
            k = k_ref[b, :, cols]
            vb = v_ref[b, :, cols].astype(BF16)
            s = state_ref[b * RET_HEADS + h]
            inner = _nt_dot(q.astype(BF16), k.astype(BF16)) * decay_ref[h]
            o = _dot(inner.astype(BF16), vb) + _dot((q * qdec_ref[h]).astype(BF16), s.astype(BF16))
            state_ref[b * RET_HEADS + h] = (math.exp(chunk * LOG_G[h]) * s
                                            + _tn_dot((k * kdec_ref[h]).astype(BF16), vb))
            y_ref[b, :, cols] = _group_norm_gate(o, gn_ref[:, cols], rg_ref[b, :, cols])

    @pl.when(c == pl.num_programs(0) - 1)
    def _fin():
        sfin_ref[...] = state_ref[...]


def _retention_prompt(rq, rk, rv, rg, gn, batch, seq):
    nc = seq // CHUNK
    blk = pl.BlockSpec((batch, CHUNK, SEG_W), lambda c: (0, c, 0))
    sq_mat = pltpu.VMEM((RET_HEADS, CHUNK, CHUNK), F32)
    return pl.pallas_call(
        functools.partial(_ret_prompt_kernel, batch=batch, chunk=CHUNK),
        grid=(nc,),
        in_specs=[blk, blk, blk, blk, pl.BlockSpec((1, SEG_W), lambda c: (0, 0))],
        out_specs=[blk, pl.BlockSpec((batch * RET_HEADS, RET_D, RET_D), lambda c: (0, 0, 0))],
        out_shape=[jax.ShapeDtypeStruct((batch, seq, SEG_W), F32),
                   jax.ShapeDtypeStruct((batch * RET_HEADS, RET_D, RET_D), F32)],
        scratch_shapes=[pltpu.VMEM((batch * RET_HEADS, RET_D, RET_D), F32), sq_mat, sq_mat, sq_mat],
        compiler_params=pltpu.CompilerParams(dimension_semantics=("arbitrary",), vmem_limit_bytes=VMEM_LIMIT),
        name="retention_prompt",
    )(rq.reshape(batch, seq, SEG_W), rk.reshape(batch, seq, SEG_W), rv.reshape(batch, seq, SEG_W),
      rg.reshape(batch, seq, SEG_W), gn)


def _ret_sample_kernel(q_ref, k_ref, v_ref, rg_ref, gn_ref, s0_ref, y_ref, s1_ref, *, bb, dt):
    grp = 16 // dt
    rows = grp * dt
    ri = lax.broadcasted_iota(jnp.int32, (rows, rows), 0)
    ci = lax.broadcasted_iota(jnp.int32, (rows, rows), 1)
    diff = (ri % dt - ci % dt).astype(F32)
    keep = (ri // dt == ci // dt) & (diff >= 0)
    rn = (lax.broadcasted_iota(jnp.int32, (rows, RET_D), 0) % dt).astype(F32)
    rb = lax.broadcasted_iota(jnp.int32, (rows, RET_D), 0) // dt
    for g in range(bb // grp):
        rsl = slice(g * rows, (g + 1) * rows)
        for h in range(RET_HEADS):
            cols = slice(h * RET_D, (h + 1) * RET_D)
            lg = LOG_G[h]
            decay = jnp.where(keep, jnp.exp(jnp.where(keep, diff, 0.0) * lg), 0.0)
            q = q_ref[rsl, cols]
            k = k_ref[rsl, cols]
            vb = v_ref[rsl, cols].astype(BF16)
            inner = _nt_dot(q.astype(BF16), k.astype(BF16)) * decay
            o = _dot(inner.astype(BF16), vb)
            qd = (q * jnp.exp((rn + 1.0) * lg)).astype(BF16)
            kd = k * jnp.exp((dt - 1.0 - rn) * lg)
            for j in range(grp):
                b = g * grp + j
                s0 = s0_ref[b, h]
                o = o + jnp.where(rb == j, _dot(qd, s0.astype(BF16)), 0.0)
                kdj = jnp.where(rb == j, kd, 0.0).astype(BF16)
                s1_ref[b, h] = math.exp(dt * lg) * s0 + _tn_dot(kdj, vb)
            y_ref[rsl, cols] = _group_norm_gate(o, gn_ref[:, cols], rg_ref[rsl, cols])


def _retention_sample(rq, rk, rv, rg, gn, state, dt):
    n = rq.shape[0]
    db = n // dt
    bb = 8
    blk = pl.BlockSpec((bb * dt, SEG_W), lambda i: (i, 0))
    st = pl.BlockSpec((bb, RET_HEADS, RET_D, RET_D), lambda i: (i, 0, 0, 0))
    return pl.pallas_call(
        functools.partial(_ret_sample_kernel, bb=bb, dt=dt),
        grid=(db // bb,),
        in_specs=[blk, blk, blk, blk, pl.BlockSpec((1, SEG_W), lambda i: (0, 0)), st],
        out_specs=[blk, st],
        out_shape=[jax.ShapeDtypeStruct((n, SEG_W), F32), jax.ShapeDtypeStruct(state.shape, F32)],
        compiler_params=pltpu.CompilerParams(dimension_semantics=("parallel",), vmem_limit_bytes=VMEM_LIMIT),
        name="retention_sample",
    )(rq, rk, rv, rg, gn, state)


def _tri(n):
    r = lax.broadcasted_iota(jnp.int32, (n, n), 0)
    c = lax.broadcasted_iota(jnp.int32, (n, n), 1)
    return jnp.where(r > c, 1.0, 0.0).astype(BF16)


def _softplus2(z, mask):
    neg_abs = lax.bitcast_convert_type(lax.bitcast_convert_type(z, jnp.uint32) | jnp.uint32(0x80000000), F32)
    sp = jnp.maximum(z, 0.0) + jnp.log2(1.0 + jnp.exp2(neg_abs))
    return sp if mask is None else jnp.where(mask, sp, 0.0)


def _sp_suffix(sp, tri):
    return sp + _dot(sp.astype(BF16), tri)


def _sb_weights(z, suffix, carry, mask):
    a = jnp.exp2(z - (suffix + carry))
    return a if mask is None else jnp.where(mask, a, 0.0)


def _head_rows_mask(rows):
    r = lax.broadcasted_iota(jnp.int32, (rows, SEG_W), 0)
    c = lax.broadcasted_iota(jnp.int32, (rows, SEG_W), 1)
    return c // SB_DH == r % SB_HEADS


def _span_logits(qbd, bias, kts):
    return [_dot(qbd, kt) + bias for kt in kts]


def _span_suffixes(zs, mask):
    tri = _tri(zs[0].shape[1])
    return [_sp_suffix(_softplus2(z, mask), tri) for z in zs]


def _span_accumulate(zs, suffixes, vts, carry, acc, mask):
    for z, suffix, vt in zip(zs, suffixes, vts):
        a = _sb_weights(z, suffix, carry, mask)
        acc = acc + _nt_dot(a.astype(BF16), vt)
        carry = carry + suffix[:, 0:1]
    return carry, acc


def _diag_pair_split(pairs, nd, cap):
    split = []
    for _ in range(nd):
        split.append(min(cap, pairs - sum(split)))
    assert sum(split) == pairs
    return split


def _sb_fused_kernel(pt_ref, bias_ref, q_ref, k_ref, v_ref, sg_ref, qs_ref, bcol_ref, ck_ref, cv_ref,
                     o_ref, os_ref,
                     qz_ref, acc_ref, carry_ref, kbuf, vbuf, ksem, vsem, qbd_ref, sacc_ref, scarry_ref,
                     *, tq, tk, n_pages, ppi, diag_pairs):
    hp = pl.program_id(1)
    qi = pl.program_id(2)
    nd = tq // tk
    rows = qbd_ref.shape[0]
    dt = rows // SB_HEADS
    pairs = n_pages // 2
    ipe = pairs // ppi
    group = pl.program_id(0) * pl.num_programs(1) + hp
    epg = os_ref.shape[0]
    first = qi * qi

    def page_copies(le, pair, slot, j):
        cps = []
        for half in range(2):
            pg = pt_ref[group * epg + le, n_pages - 2 - 2 * pair + half]
            cps.append(pltpu.make_async_copy(ck_ref.at[pg], kbuf.at[slot, 2 * j + half], ksem.at[slot]))
            cps.append(pltpu.make_async_copy(cv_ref.at[pg], vbuf.at[slot, 2 * j + half], vsem.at[slot]))
        return cps

    def fetch(le, pair0, n, slot):
        for j in range(n):
            for cp in page_copies(le, pair0 + j, slot, j):
                cp.start()

    def arrive(le, pair0, n, slot):
        for j in range(n):
            for cp in page_copies(le, pair0 + j, slot, j):
                cp.wait()

    def seq_begin(le):
        q = qs_ref[le]
        qbd_ref[...] = jnp.where(_head_rows_mask(rows), q, jnp.zeros_like(q))
        sacc_ref[...] = jnp.zeros_like(sacc_ref)
        scarry_ref[...] = jnp.zeros_like(scarry_ref)

    def span(buf, slot, j):
        return jnp.concatenate([buf[slot, 2 * j].astype(BF16), buf[slot, 2 * j + 1].astype(BF16)], axis=1)

    def seq_end(le):
        own = jnp.where(_head_rows_mask(rows), sacc_ref[...], 0.0)
        os_ref[le] = jnp.sum(own.reshape(dt, SB_HEADS, SEG_W), axis=1)

    lane = lax.broadcasted_iota(jnp.int32, (tq, LANES), 1)
    q = q_ref[...]
    qz_ref[0] = jnp.where(lane < SB_DH, q, jnp.zeros_like(q))
    qz_ref[1] = jnp.where(lane >= SB_DH, q, jnp.zeros_like(q))
    acc_ref[...] = jnp.zeros_like(acc_ref)
    carry_ref[...] = jnp.zeros_like(carry_ref)
    tri = _tri(tk)
    bias = [bias_ref[2 * hp] * LOG2E, bias_ref[2 * hp + 1] * LOG2E]
    q0 = qi * tq

    def block(k0, r0, mask, slot, n):
        if n:
            zs_s = _span_logits(qbd_ref[...], bcol_ref[...], [span(kbuf, slot, j) for j in range(n)])
            scarry, sacc = _span_accumulate(zs_s, _span_suffixes(zs_s, None),
                                            [span(vbuf, slot, j) for j in range(n)],
                                            scarry_ref[...], sacc_ref[...], None)
            scarry_ref[...] = scarry
            sacc_ref[...] = sacc
        kt = k_ref[0, :, pl.ds(k0, tk)]
        vb = v_ref[pl.ds(k0, tk), :]
        for h in range(2):
            z = _dot(qz_ref[h, r0:, :], kt) + bias[h]
            suffix = _sp_suffix(_softplus2(z, mask), tri)
            carry = carry_ref[h, r0:, :]
            a = _sb_weights(z, suffix, carry, mask)
            carry_ref[h, r0:, :] = carry + suffix[:, 0:1]
            acc_ref[h, r0:, :] += _dot(a.astype(BF16), vb)

    seq_begin(first)
    fetch(first, 0, diag_pairs[nd - 1], 0)
    done = 0
    for k, d in enumerate(reversed(range(nd))):
        n, slot = diag_pairs[d], k % 2
        if d > 0:
            fetch(first, done + n, diag_pairs[d - 1], 1 - slot)
        else:
            @pl.when(qi > 0)
            def _():
                fetch(first + 1, 0, ppi, 1 - slot)
        arrive(first, done, n, slot)
        rl = lax.broadcasted_iota(jnp.int32, (tq - d * tk, tk), 0)
        cl = lax.broadcasted_iota(jnp.int32, (tq - d * tk, tk), 1)
        block(pl.multiple_of(q0 + d * tk, tk), d * tk, cl < rl, slot, n)
        done += n
    seq_end(first)

    def body(i, _):
        slot = (nd + i) % 2
        le = first + 1 + i // ipe
        part = i % ipe

        @pl.when(part == 0)
        def _():
            seq_begin(le)

        @pl.when(i + 1 < qi * nd)
        def _():
            fetch(first + 1 + (i + 1) // ipe, ((i + 1) % ipe) * ppi, ppi, 1 - slot)

        arrive(le, part * ppi, ppi, slot)
        block(pl.multiple_of(q0 - (i + 1) * tk, tk), 0, None, slot, ppi)

        @pl.when(part == ipe - 1)
        def _():
            seq_end(le)
        return 0

    lax.fori_loop(0, qi * nd, body, 0)
    o = jnp.where(lane < SB_DH, acc_ref[0], acc_ref[1])
    o_ref[...] = o * _silu(sg_ref[...])


def _sb_fused(page_table, sb_bias, sqb, skb_t, svb, sg, q_rep, bias_col, cache_kt, cache_vt, batch, seq, tq, tk):
    nq = seq // tq
    nd = tq // tk
    hpairs = SB_HEADS // 2
    db, n_pages = page_table.shape
    rows = q_rep.shape[1]
    page = cache_kt.shape[2]
    pairs = n_pages // 2
    ppi = 8
    epg = nq * nq
    assert n_pages % 2 == 0 and pairs % ppi == 0 and nd % (pairs // ppi) == 0 and nd // (pairs // ppi) == 2
    assert db == batch * hpairs * epg, "sample batch must tile over (batch, head pair, query block)"
    diag_pairs = _diag_pair_split(pairs, nd, ppi)
    qblk = pl.BlockSpec((tq, LANES), lambda b, hp, i, pt: (b * nq + i, hp))
    ktblk = pl.BlockSpec((1, LANES, seq), lambda b, hp, i, pt: (b * hpairs + hp, 0, 0), pipeline_mode=pl.Buffered(1))
    vblk = pl.BlockSpec((seq, LANES), lambda b, hp, i, pt: (b, hp), pipeline_mode=pl.Buffered(1))
    grp = lambda shape: pl.BlockSpec((epg,) + shape, lambda b, hp, i, pt: (b * hpairs + hp, 0, 0))
    grid_spec = pltpu.PrefetchScalarGridSpec(
        num_scalar_prefetch=1,
        grid=(batch, hpairs, nq),
        in_specs=[pl.BlockSpec(memory_space=pltpu.SMEM), qblk, ktblk, vblk, qblk,
                  grp((rows, SEG_W)), pl.BlockSpec((rows, 1), lambda b, hp, i, pt: (0, 0)),
                  pl.BlockSpec(memory_space=pl.ANY), pl.BlockSpec(memory_space=pl.ANY)],
        out_specs=[qblk, grp((rows // SB_HEADS, SEG_W))],
        scratch_shapes=[pltpu.VMEM((2, tq, LANES), BF16), pltpu.VMEM((2, tq, LANES), F32),
                        pltpu.VMEM((2, tq, 1), F32),
                        pltpu.VMEM((2, 2 * ppi, SEG_W, page), F32), pltpu.VMEM((2, 2 * ppi, SEG_W, page), F32),
                        pltpu.SemaphoreType.DMA((2,)), pltpu.SemaphoreType.DMA((2,)),
                        pltpu.VMEM((rows, SEG_W), BF16), pltpu.VMEM((rows, SEG_W), F32),
                        pltpu.VMEM((rows, 1), F32)],
    )
    return pl.pallas_call(
        functools.partial(_sb_fused_kernel, tq=tq, tk=tk, n_pages=n_pages, ppi=ppi, diag_pairs=diag_pairs),
        grid_spec=grid_spec,
        out_shape=[jax.ShapeDtypeStruct((batch * seq, SEG_W), F32),
                   jax.ShapeDtypeStruct((db, rows // SB_HEADS, SEG_W), F32)],
        compiler_params=pltpu.CompilerParams(dimension_semantics=("arbitrary", "arbitrary", "arbitrary"),
                                             vmem_limit_bytes=FUSED_VMEM_LIMIT),
        name="sb_fused",
    )(page_table, sb_bias, sqb, skb_t, svb, sg, q_rep, bias_col, cache_kt, cache_vt)


def _sb_sample_finish_kernel(bias_ref, q_ref, kn_ref, vn_ref, past_ref, sg_ref, o_ref, *, bb, page, dt):
    rows = dt * SB_HEADS
    own_rows = _head_rows_mask(rows)
    rq = lax.broadcasted_iota(jnp.int32, (rows, page), 0) // SB_HEADS
    cm = lax.broadcasted_iota(jnp.int32, (rows, page), 1)
    zero_carry = jnp.zeros((rows, 1), F32)
    for e in range(bb):
        q = q_ref[e]
        qbd = jnp.where(own_rows, q, jnp.zeros_like(q))
        zs = _span_logits(qbd, bias_ref[...], [kn_ref[e]])
        c0, acc = _span_accumulate(zs, _span_suffixes(zs, cm < rq), [vn_ref[e]], zero_carry,
                                   jnp.zeros((rows, SEG_W), F32), cm < rq)
        past = jnp.broadcast_to(past_ref[e][:, None, :], (dt, SB_HEADS, SEG_W)).reshape(rows, SEG_W)
        own = jnp.where(own_rows, acc + jnp.exp2(-c0) * past, 0.0)
        o = jnp.sum(own.reshape(dt, SB_HEADS, SEG_W), axis=1)
        o_ref[e] = o * _silu(sg_ref[e])


def _sb_sample_finish(bias_col, q_rep, k_new, v_new, past, sg):
    db, rows, _ = q_rep.shape
    page = k_new.shape[2]
    dt = sg.shape[1]
    bb = 8
    per_b = lambda blk: pl.BlockSpec((bb,) + blk, lambda i: (i, 0, 0))
    return pl.pallas_call(
        functools.partial(_sb_sample_finish_kernel, bb=bb, page=page, dt=dt),
        grid=(db // bb,),
        in_specs=[pl.BlockSpec((rows, 1), lambda i: (0, 0)),
                  per_b((rows, SEG_W)), per_b((SEG_W, page)), per_b((SEG_W, page)),
                  per_b((dt, SEG_W)), per_b((dt, SEG_W))],
        out_specs=per_b((dt, SEG_W)),
        out_shape=jax.ShapeDtypeStruct((db, dt, SEG_W), F32),
        compiler_params=pltpu.CompilerParams(dimension_semantics=("parallel",), vmem_limit_bytes=VMEM_LIMIT),
        name="sb_sample_finish",
    )(bias_col, q_rep, k_new, v_new, past, sg)


def _out_kernel(yr_ref, ys_ref, h_ref, p_ref, wo_ref, wp_ref, wg_ref, gp_ref, gf_ref, o_ref):
    y = _dot(yr_ref[...].astype(BF16), wo_ref[:SEG_W, :]) + _dot(ys_ref[...].astype(BF16), wo_ref[SEG_W:, :])
    h1 = h_ref[...] + y
    e = _dot(p_ref[...].astype(BF16), wp_ref[...])
    e = (e * lax.rsqrt(jnp.mean(e * e, axis=-1, keepdims=True) + EPS)) * gp_ref[...]
    gate = 1.0 / (1.0 + jnp.exp(-_dot(h1.astype(BF16), wg_ref[...])))
    h2 = h1 + e * gate
    o_ref[...] = (h2 * lax.rsqrt(jnp.mean(h2 * h2, axis=-1, keepdims=True) + EPS)) * gf_ref[...]


def _out_proj(yr, ys, h, p, wo_bf, wp_bf, wg_bf, g_ple, g_final, tm):
    n, d = h.shape
    d_ple = p.shape[1]
    row = lambda i: (i, 0)
    const = lambda i: (0, 0)
    return pl.pallas_call(
        _out_kernel,
        grid=(n // tm,),
        in_specs=[pl.BlockSpec((tm, SEG_W), row), pl.BlockSpec((tm, SEG_W), row), pl.BlockSpec((tm, d), row),
                  pl.BlockSpec((tm, d_ple), row), pl.BlockSpec((2 * SEG_W, d), const),
                  pl.BlockSpec((d_ple, d), const), pl.BlockSpec((d, d), const),
                  pl.BlockSpec((1, d), const), pl.BlockSpec((1, d), const)],
        out_specs=pl.BlockSpec((tm, d), row),
        out_shape=jax.ShapeDtypeStruct((n, d), F32),
        compiler_params=pltpu.CompilerParams(dimension_semantics=("parallel",), vmem_limit_bytes=VMEM_LIMIT),
        name="out_proj_ple_norm",
    )(yr, ys, h, p, wo_bf, wp_bf, wg_bf, g_ple, g_final)


def kernel(x_prompt, x_sample, cache_sb_k, cache_sb_v, state_ret, page_table, p_prompt, p_sample,
           g_norm, w_in, sb_bias, g_ret_gn, w_out, w_ple, g_ple, w_ple_gate, g_final):
    b, t, d = x_prompt.shape
    db, dt, _ = x_sample.shape
    depth = w_in.shape[0]
    assert depth == 1, "single-layer trunk"
    n_pool, page = cache_sb_k.shape[1], cache_sb_k.shape[2]
    past = page_table.shape[1] * page
    tm_p = min(256, t)
    tm_s = min(256, db * dt)

    half = RET_D // 2
    inv = ROPE_BASE ** (-jnp.arange(half, dtype=F32) / half)
    inv128 = jnp.concatenate([inv, inv])[None, :]
    cos_p, sin_p = _rope_table(inv128, t, 0, t)
    cos_s, sin_s = _rope_table(inv128, tm_s, past, dt)

    w_bf = w_in[0].astype(BF16)
    wkv_t = w_bf[:, 5 * SEG_W:6 * SEG_W].T
    gn = g_norm[0][None, :]
    ggn = g_ret_gn[0][None, :]
    wo_bf = w_out[0].astype(BF16)
    wp_bf = w_ple[0].astype(BF16)
    wg_bf = w_ple_gate[0].astype(BF16)
    gp = g_ple[0][None, :]
    gf = g_final[None, :]

    xp = x_prompt.reshape(b * t, d)
    rq, rk, rv, rg, sqb, sk_t, sv_t, sg, skb_t, svb = _project(xp, gn, w_bf, wkv_t, cos_p, sin_p, t // tm_p, tm_p, seq=t)
    xs = x_sample.reshape(db * dt, d)
    rq_s, rk_s, rv_s, rg_s, sqb_s, sk_s, sv_s, sg_s, skb_s, svb_s = _project(xs, gn, w_bf, wkv_t, cos_s, sin_s, 1, tm_s)

    q_rep = jnp.repeat(sqb_s.reshape(db, dt, SEG_W), SB_HEADS, axis=1)
    bias_col = jnp.tile(sb_bias[0] * LOG2E, dt)[:, None]
    cache_kt = jnp.transpose(cache_sb_k[0], (0, 2, 3, 1)).reshape(n_pool, SEG_W, page)
    cache_vt = jnp.transpose(cache_sb_v[0], (0, 2, 3, 1)).reshape(n_pool, SEG_W, page)
    tq = min(2048, t)
    ys, past_s = _sb_fused(page_table, sb_bias[0], sqb, skb_t.reshape(b * (SB_HEADS // 2), LANES, t), svb, sg,
                           q_rep, bias_col, cache_kt, cache_vt, b, t, tq, min(256, tq))
    pad = ((0, 0), (0, 0), (0, page - dt))
    k_new = jnp.pad(jnp.swapaxes(skb_s.reshape(db, dt, SEG_W), 1, 2), pad)
    v_new = jnp.pad(jnp.swapaxes(svb_s.reshape(db, dt, SEG_W), 1, 2), pad)
    ys_s = _sb_sample_finish(bias_col, q_rep, k_new, v_new, past_s, sg_s.reshape(db, dt, SEG_W))

    yr, s_fin = _retention_prompt(rq, rk, rv, rg, ggn, b, t)
    yr_s, s_new = _retention_sample(rq_s, rk_s, rv_s, rg_s, ggn, state_ret[0], dt)
    y_prompt = _out_proj(yr.reshape(b * t, SEG_W), ys, xp, p_prompt[0].reshape(b * t, -1),
                         wo_bf, wp_bf, wg_bf, gp, gf, min(512, b * t))
    y_sample = _out_proj(yr_s, ys_s.reshape(db * dt, SEG_W), xs, p_sample[0].reshape(db * dt, -1),
                         wo_bf, wp_bf, wg_bf, gp, gf, min(512, db * dt))

    return (y_prompt.reshape(b, t, d),
            y_sample.reshape(db, dt, d),
            jnp.transpose(sk_t.reshape(1, b, SB_HEADS, SB_DH, t), (0, 1, 4, 2, 3)),
            jnp.transpose(sv_t.reshape(1, b, SB_HEADS, SB_DH, t), (0, 1, 4, 2, 3)),
            sk_s.reshape(1, db, dt, SB_HEADS, SB_DH),
            sv_s.reshape(1, db, dt, SB_HEADS, SB_DH),
            s_fin.reshape(1, b, RET_HEADS, RET_D, RET_D),
            s_new[None])
```

```python
import functools
import math

import numpy as np
import jax
import jax.numpy as jnp
from jax import lax
from jax.experimental import pallas as pl
from jax.experimental.pallas import tpu as pltpu

F32 = jnp.float32
BF16 = jnp.bfloat16

RET_HEADS = 4
RET_D = 128
SB_HEADS = 8
SB_DH = 64
SEG_W = 512
CHUNK = 128
ROPE_BASE = 10000.0
EPS = 1e-6
LOG2E = math.log2(math.e)
LANES = 128
VMEM_LIMIT = 48 * 1024 * 1024
FUSED_VMEM_LIMIT = 56 * 1024 * 1024

LOG_G = [float(np.log(np.float32(1.0) - np.float32(2.0) ** np.float32(-5.0 - h))) for h in range(RET_HEADS)]


def _nt_dot(a, b):
    return lax.dot_general(a, b, (((1,), (1,)), ((), ())), preferred_element_type=F32)


def _tn_dot(a, b):
    return lax.dot_general(a, b, (((0,), (0,)), ((), ())), preferred_element_type=F32)


def _dot(a, b):
    return jnp.dot(a, b, preferred_element_type=F32)


def _silu(x):
    return x / (1.0 + jnp.exp(-x))


def _rope_kernel(inv_ref, cos_ref, sin_ref, *, base, period, rows):
    r = lax.broadcasted_iota(jnp.int32, (rows, LANES), 0) + pl.program_id(0) * rows
    lane = lax.broadcasted_iota(jnp.int32, (rows, LANES), 1)
    pos = (base + r % period).astype(F32)
    ang = pos * inv_ref[...]
    cos_ref[...] = jnp.cos(ang)
    sin_ref[...] = jnp.where(lane < LANES // 2, -1.0, 1.0) * jnp.sin(ang)


def _rope_table(inv128, n_rows, base, period):
    rows = min(n_rows, 512)
    return pl.pallas_call(
        functools.partial(_rope_kernel, base=base, period=period, rows=rows),
        grid=(n_rows // rows,),
        in_specs=[pl.BlockSpec((1, LANES), lambda i: (0, 0))],
        out_specs=[pl.BlockSpec((rows, LANES), lambda i: (i, 0))] * 2,
        out_shape=[jax.ShapeDtypeStruct((n_rows, LANES), F32)] * 2,
        name="rope_table",
    )(inv128)


def _proj_kernel(x_ref, g_ref, w_ref, wkv_t_ref, cos_ref, sin_ref,
                 rq_ref, rk_ref, rv_ref, rg_ref, sqb_ref, sk_ref, sv_ref, sg_ref, skb_ref, svb_ref, *, kv_transposed):
    x = x_ref[...]
    ms = jnp.mean(x * x, axis=-1, keepdims=True)
    xn = ((x * lax.rsqrt(ms + EPS)) * g_ref[...]).astype(BF16)
    cos = cos_ref[...]
    sin = sin_ref[...]

    def seg(j):
        return _dot(xn, w_ref[:, j * SEG_W:(j + 1) * SEG_W])

    def rotary_store(p, out_ref, scale):
        for h in range(RET_HEADS):
            ph = p[:, h * RET_D:(h + 1) * RET_D]
            rot = ph * cos + pltpu.roll(ph, RET_D // 2, axis=1) * sin
            out_ref[:, h * RET_D:(h + 1) * RET_D] = rot if scale is None else rot * scale

    rotary_store(seg(0), rq_ref, None)
    rotary_store(seg(1), rk_ref, RET_D ** -0.5)
    rv_ref[...] = seg(2)
    rg_ref[...] = seg(3)
    sqb_ref[...] = (seg(4) * (LOG2E * SB_DH ** -0.5)).astype(BF16)
    if kv_transposed:
        sk_t = _nt_dot(wkv_t_ref[...], xn)
        sk_ref[0] = sk_t
        skb_ref[0] = sk_t.astype(BF16)
        sv = seg(6)
        sv_ref[0] = sv.T
        svb_ref[...] = sv.astype(BF16)
    else:
        sk = seg(5)
        sk_ref[...] = sk
        skb_ref[...] = sk.astype(BF16)
        sv = seg(6)
        sv_ref[...] = sv
        svb_ref[...] = sv.astype(BF16)
    sg_ref[...] = seg(7)


def _project(x2d, g_norm, w_bf, wkv_t, cos_t, sin_t, table_period_blocks, tm, seq=None):
    n, d = x2d.shape
    d_in = w_bf.shape[1]
    row = lambda i: (i, 0)
    const2 = lambda i: (0, 0)
    f32_out = jax.ShapeDtypeStruct((n, SEG_W), F32)
    bf_out = jax.ShapeDtypeStruct((n, SEG_W), BF16)
    seg_spec = pl.BlockSpec((tm, SEG_W), row)
    tab_spec = pl.BlockSpec((tm, LANES), lambda i: (i % table_period_blocks, 0))
    if seq is None:
        kv_f32, kv_bf, kv_spec = f32_out, bf_out, seg_spec
    else:
        per_b = seq // tm
        kv_f32 = jax.ShapeDtypeStruct((n // seq, SEG_W, seq), F32)
        kv_bf = jax.ShapeDtypeStruct((n // seq, SEG_W, seq), BF16)
        kv_spec = pl.BlockSpec((1, SEG_W, tm), lambda i: (i // per_b, 0, i % per_b))
    return pl.pallas_call(
        functools.partial(_proj_kernel, kv_transposed=seq is not None),
        grid=(n // tm,),
        in_specs=[pl.BlockSpec((tm, d), row),
                  pl.BlockSpec((1, d), const2),
                  pl.BlockSpec((d, d_in), const2),
                  pl.BlockSpec((SEG_W, d), const2),
                  tab_spec, tab_spec],
        out_specs=[seg_spec, seg_spec, seg_spec, seg_spec, seg_spec, kv_spec, kv_spec, seg_spec, kv_spec, seg_spec],
        out_shape=[f32_out, f32_out, f32_out, f32_out, bf_out, kv_f32, kv_f32, f32_out, kv_bf, bf_out],
        compiler_params=pltpu.CompilerParams(dimension_semantics=("parallel",), vmem_limit_bytes=VMEM_LIMIT),
        name="rmsnorm_in_proj",
    )(x2d, g_norm, w_bf, wkv_t, cos_t, sin_t)


def _group_norm_gate(o, gn, rg):
    mu = jnp.mean(o, axis=-1, keepdims=True)
    d = o - mu
    var = jnp.mean(d * d, axis=-1, keepdims=True)
    return (d * lax.rsqrt(var + EPS)) * gn * _silu(rg)


def _ret_prompt_kernel(q_ref, k_ref, v_ref, rg_ref, gn_ref, y_ref, sfin_ref,
                       state_ref, decay_ref, qdec_ref, kdec_ref, *, batch, chunk):
    c = pl.program_id(0)

    @pl.when(c == 0)
    def _init():
        state_ref[...] = jnp.zeros_like(state_ref)
        n = lax.broadcasted_iota(jnp.int32, (chunk, chunk), 0).astype(F32)
        m = lax.broadcasted_iota(jnp.int32, (chunk, chunk), 1).astype(F32)
        diff = n - m
        causal = diff >= 0
        for h in range(RET_HEADS):
            decay_ref[h] = jnp.where(causal, jnp.exp(jnp.where(causal, diff, 0.0) * LOG_G[h]), 0.0)
            qdec_ref[h] = jnp.exp((n + 1.0) * LOG_G[h])
            kdec_ref[h] = jnp.exp((chunk - 1.0 - n) * LOG_G[h])

    for b in range(batch):
        for h in range(RET_HEADS):
            cols = slice(h * RET_D, (h + 1) * RET_D)
            q = q_ref[b, :, cols]
            k = k_ref[b, :, cols]
            vb = v_ref[b, :, cols].astype(BF16)
            s = state_ref[b * RET_HEADS + h]
            inner = _nt_dot(q.astype(BF16), k.astype(BF16)) * decay_ref[h]
            o = _dot(inner.astype(BF16), vb) + _dot((q * qdec_ref[h]).astype(BF16), s.astype(BF16))
            state_ref[b * RET_HEADS + h] = (math.exp(chunk * LOG_G[h]) * s
                                            + _tn_dot((k * kdec_ref[h]).astype(BF16), vb))
            y_ref[b, :, cols] = _group_norm_gate(o, gn_ref[:, cols], rg_ref[b, :, cols])

    @pl.when(c == pl.num_programs(0) - 1)
    def _fin():
        sfin_ref[...] = state_ref[...]


def _retention_prompt(rq, rk, rv, rg, gn, batch, seq):
    nc = seq // CHUNK
    blk = pl.BlockSpec((batch, CHUNK, SEG_W), lambda c: (0, c, 0))
    sq_mat = pltpu.VMEM((RET_HEADS, CHUNK, CHUNK), F32)
    return pl.pallas_call(
        functools.partial(_ret_prompt_kernel, batch=batch, chunk=CHUNK),
        grid=(nc,),
        in_specs=[blk, blk, blk, blk, pl.BlockSpec((1, SEG_W), lambda c: (0, 0))],
        out_specs=[blk, pl.BlockSpec((batch * RET_HEADS, RET_D, RET_D), lambda c: (0, 0, 0))],
        out_shape=[jax.ShapeDtypeStruct((batch, seq, SEG_W), F32),
                   jax.ShapeDtypeStruct((batch * RET_HEADS, RET_D, RET_D), F32)],
        scratch_shapes=[pltpu.VMEM((batch * RET_HEADS, RET_D, RET_D), F32), sq_mat, sq_mat, sq_mat],
        compiler_params=pltpu.CompilerParams(dimension_semantics=("arbitrary",), vmem_limit_bytes=VMEM_LIMIT),
        name="retention_prompt",
    )(rq.reshape(batch, seq, SEG_W), rk.reshape(batch, seq, SEG_W), rv.reshape(batch, seq, SEG_W),
      rg.reshape(batch, seq, SEG_W), gn)


def _ret_sample_kernel(q_ref, k_ref, v_ref, rg_ref, gn_ref, s0_ref, y_ref, s1_ref, *, bb, dt):
    grp = 16 // dt
    rows = grp * dt
    ri = lax.broadcasted_iota(jnp.int32, (rows, rows), 0)
    ci = lax.broadcasted_iota(jnp.int32, (rows, rows), 1)
    diff = (ri % dt - ci % dt).astype(F32)
    keep = (ri // dt == ci // dt) & (diff >= 0)
    rn = (lax.broadcasted_iota(jnp.int32, (rows, RET_D), 0) % dt).astype(F32)
    rb = lax.broadcasted_iota(jnp.int32, (rows, RET_D), 0) // dt
    for g in range(bb // grp):
        rsl = slice(g * rows, (g + 1) * rows)
        for h in range(RET_HEADS):
            cols = slice(h * RET_D, (h + 1) * RET_D)
            lg = LOG_G[h]
            decay = jnp.where(keep, jnp.exp(jnp.where(keep, diff, 0.0) * lg), 0.0)
            q = q_ref[rsl, cols]
            k = k_ref[rsl, cols]
            vb = v_ref[rsl, cols].astype(BF16)
            inner = _nt_dot(q.astype(BF16), k.astype(BF16)) * decay
            o = _dot(inner.astype(BF16), vb)
            qd = (q * jnp.exp((rn + 1.0) * lg)).astype(BF16)
            kd = k * jnp.exp((dt - 1.0 - rn) * lg)
            for j in range(grp):
                b = g * grp + j
                s0 = s0_ref[b, h]
                o = o + jnp.where(rb == j, _dot(qd, s0.astype(BF16)), 0.0)
                kdj = jnp.where(rb == j, kd, 0.0).astype(BF16)
                s1_ref[b, h] = math.exp(dt * lg) * s0 + _tn_dot(kdj, vb)
            y_ref[rsl, cols] = _group_norm_gate(o, gn_ref[:, cols], rg_ref[rsl, cols])


def _retention_sample(rq, rk, rv, rg, gn, state, dt):
    n = rq.shape[0]
    db = n // dt
    bb = 8
    blk = pl.BlockSpec((bb * dt, SEG_W), lambda i: (i, 0))
    st = pl.BlockSpec((bb, RET_HEADS, RET_D, RET_D), lambda i: (i, 0, 0, 0))
    return pl.pallas_call(
        functools.partial(_ret_sample_kernel, bb=bb, dt=dt),
        grid=(db // bb,),
        in_specs=[blk, blk, blk, blk, pl.BlockSpec((1, SEG_W), lambda i: (0, 0)), st],
        out_specs=[blk, st],
        out_shape=[jax.ShapeDtypeStruct((n, SEG_W), F32), jax.ShapeDtypeStruct(state.shape, F32)],
        compiler_params=pltpu.CompilerParams(dimension_semantics=("parallel",), vmem_limit_bytes=VMEM_LIMIT),
        name="retention_sample",
    )(rq, rk, rv, rg, gn, state)


def _tri(n):
    r = lax.broadcasted_iota(jnp.int32, (n, n), 0)
    c = lax.broadcasted_iota(jnp.int32, (n, n), 1)
    return jnp.where(r > c, 1.0, 0.0).astype(BF16)


def _softplus2(z, mask):
    neg_abs = lax.bitcast_convert_type(lax.bitcast_convert_type(z, jnp.uint32) | jnp.uint32(0x80000000), F32)
    sp = jnp.maximum(z, 0.0) + jnp.log2(1.0 + jnp.exp2(neg_abs))
    return sp if mask is None else jnp.where(mask, sp, 0.0)


def _sp_suffix(sp, tri):
    return sp + _dot(sp.astype(BF16), tri)


def _sb_weights(z, suffix, carry, mask):
    a = jnp.exp2(z - (suffix + carry))
    return a if mask is None else jnp.where(mask, a, 0.0)


def _head_rows_mask(rows):
    r = lax.broadcasted_iota(jnp.int32, (rows, SEG_W), 0)
    c = lax.broadcasted_iota(jnp.int32, (rows, SEG_W), 1)
    return c // SB_DH == r % SB_HEADS


def _span_logits(qbd, bias, kts):
    return [_dot(qbd, kt) + bias for kt in kts]


def _span_suffixes(zs, mask):
    tri = _tri(zs[0].shape[1])
    return [_sp_suffix(_softplus2(z, mask), tri) for z in zs]


def _span_accumulate(zs, suffixes, vts, carry, acc, mask):
    for z, suffix, vt in zip(zs, suffixes, vts):
        a = _sb_weights(z, suffix, carry, mask)
        acc = acc + _nt_dot(a.astype(BF16), vt)
        carry = carry + suffix[:, 0:1]
    return carry, acc


def _diag_pair_split(pairs, nd, cap):
    split = []
    for _ in range(nd):
        split.append(min(cap, pairs - sum(split)))
    assert sum(split) == pairs
    return split


def _sb_fused_kernel(pt_ref, bias_ref, q_ref, k_ref, v_ref, sg_ref, qs_ref, bcol_ref, ck_ref, cv_ref,
                     o_ref, os_ref,
                     qz_ref, acc_ref, carry_ref, kbuf, vbuf, ksem, vsem, qbd_ref, sacc_ref, scarry_ref,
                     *, tq, tk, n_pages, ppi, diag_pairs):
    hp = pl.program_id(1)
    qi = pl.program_id(2)
    nd = tq // tk
    rows = qbd_ref.shape[0]
    dt = rows // SB_HEADS
    pairs = n_pages // 2
    ipe = pairs // ppi
    group = pl.program_id(0) * pl.num_programs(1) + hp
    epg = os_ref.shape[0]
    first = qi * qi

    def page_copies(le, pair, slot, j):
        cps = []
        for half in range(2):
            pg = pt_ref[group * epg + le, n_pages - 2 - 2 * pair + half]
            cps.append(pltpu.make_async_copy(ck_ref.at[pg], kbuf.at[slot, 2 * j + half], ksem.at[slot]))
            cps.append(pltpu.make_async_copy(cv_ref.at[pg], vbuf.at[slot, 2 * j + half], vsem.at[slot]))
        return cps

    def fetch(le, pair0, n, slot):
        for j in range(n):
            for c, cp in enumerate(page_copies(le, pair0 + j, slot, j)):
                cp.start(priority=c % 2)

    def arrive(le, pair0, n, slot):
        for j in range(n):
            for cp in page_copies(le, pair0 + j, slot, j):
                cp.wait()

    def seq_begin(le):
        q = qs_ref[le]
        qbd_ref[...] = jnp.where(_head_rows_mask(rows), q, jnp.zeros_like(q))
        sacc_ref[...] = jnp.zeros_like(sacc_ref)
        scarry_ref[...] = jnp.zeros_like(scarry_ref)

    def span(buf, slot, j):
        return jnp.concatenate([buf[slot, 2 * j].astype(BF16), buf[slot, 2 * j + 1].astype(BF16)], axis=1)

    def seq_end(le):
        own = jnp.where(_head_rows_mask(rows), sacc_ref[...], 0.0)
        os_ref[le] = jnp.sum(own.reshape(dt, SB_HEADS, SEG_W), axis=1)

    lane = lax.broadcasted_iota(jnp.int32, (tq, LANES), 1)
    q = q_ref[...]
    qz_ref[0] = jnp.where(lane < SB_DH, q, jnp.zeros_like(q))
    qz_ref[1] = jnp.where(lane >= SB_DH, q, jnp.zeros_like(q))
    acc_ref[...] = jnp.zeros_like(acc_ref)
    carry_ref[...] = jnp.zeros_like(carry_ref)
    tri = _tri(tk)
    bias = [bias_ref[2 * hp] * LOG2E, bias_ref[2 * hp + 1] * LOG2E]
    q0 = qi * tq

    def block(k0, r0, mask, slot, n):
        if n:
            zs_s = _span_logits(qbd_ref[...], bcol_ref[...], [span(kbuf, slot, j) for j in range(n)])
            scarry, sacc = _span_accumulate(zs_s, _span_suffixes(zs_s, None),
                                            [span(vbuf, slot, j) for j in range(n)],
                                            scarry_ref[...], sacc_ref[...], None)
            scarry_ref[...] = scarry
            sacc_ref[...] = sacc
        kt = k_ref[0, :, pl.ds(k0, tk)]
        vb = v_ref[pl.ds(k0, tk), :]
        for h in range(2):
            z = _dot(qz_ref[h, r0:, :], kt) + bias[h]
            suffix = _sp_suffix(_softplus2(z, mask), tri)
            carry = carry_ref[h, r0:, :]
            a = _sb_weights(z, suffix, carry, mask)
            carry_ref[h, r0:, :] = carry + suffix[:, 0:1]
            acc_ref[h, r0:, :] += _dot(a.astype(BF16), vb)

    seq_begin(first)
    fetch(first, 0, diag_pairs[nd - 1], 0)
    done = 0
    for k, d in enumerate(reversed(range(nd))):
        n, slot = diag_pairs[d], k % 2
        if d > 0:
            fetch(first, done + n, diag_pairs[d - 1], 1 - slot)
        else:
            @pl.when(qi > 0)
            def _():
                fetch(first + 1, 0, ppi, 1 - slot)
        arrive(first, done, n, slot)
        rl = lax.broadcasted_iota(jnp.int32, (tq - d * tk, tk), 0)
        cl = lax.broadcasted_iota(jnp.int32, (tq - d * tk, tk), 1)
        block(pl.multiple_of(q0 + d * tk, tk), d * tk, cl < rl, slot, n)
        done += n
    seq_end(first)

    def body(i, _):
        slot = (nd + i) % 2
        le = first + 1 + i // ipe
        part = i % ipe

        @pl.when(part == 0)
        def _():
            seq_begin(le)

        @pl.when(i + 1 < qi * nd)
        def _():
            fetch(first + 1 + (i + 1) // ipe, ((i + 1) % ipe) * ppi, ppi, 1 - slot)

        arrive(le, part * ppi, ppi, slot)
        block(pl.multiple_of(q0 - (i + 1) * tk, tk), 0, None, slot, ppi)

        @pl.when(part == ipe - 1)
        def _():
            seq_end(le)
        return 0

    lax.fori_loop(0, qi * nd, body, 0)
    o = jnp.where(lane < SB_DH, acc_ref[0], acc_ref[1])
    o_ref[...] = o * _silu(sg_ref[...])


def _sb_fused(page_table, sb_bias, sqb, skb_t, svb, sg, q_rep, bias_col, cache_kt, cache_vt, batch, seq, tq, tk):
    nq = seq // tq
    nd = tq // tk
    hpairs = SB_HEADS // 2
    db, n_pages = page_table.shape
    rows = q_rep.shape[1]
    page = cache_kt.shape[2]
    pairs = n_pages // 2
    ppi = 8
    epg = nq * nq
    assert n_pages % 2 == 0 and pairs % ppi == 0 and nd % (pairs // ppi) == 0 and nd // (pairs // ppi) == 2
    assert db == batch * hpairs * epg, "sample batch must tile over (batch, head pair, query block)"
    diag_pairs = _diag_pair_split(pairs, nd, ppi)
    qblk = pl.BlockSpec((tq, LANES), lambda b, hp, i, pt: (b * nq + i, hp))
    ktblk = pl.BlockSpec((1, LANES, seq), lambda b, hp, i, pt: (b * hpairs + hp, 0, 0), pipeline_mode=pl.Buffered(1))
    vblk = pl.BlockSpec((seq, LANES), lambda b, hp, i, pt: (b, hp), pipeline_mode=pl.Buffered(1))
    grp = lambda shape: pl.BlockSpec((epg,) + shape, lambda b, hp, i, pt: (b * hpairs + hp, 0, 0))
    grid_spec = pltpu.PrefetchScalarGridSpec(
        num_scalar_prefetch=1,
        grid=(batch, hpairs, nq),
        in_specs=[pl.BlockSpec(memory_space=pltpu.SMEM), qblk, ktblk, vblk, qblk,
                  grp((rows, SEG_W)), pl.BlockSpec((rows, 1), lambda b, hp, i, pt: (0, 0)),
                  pl.BlockSpec(memory_space=pl.ANY), pl.BlockSpec(memory_space=pl.ANY)],
        out_specs=[qblk, grp((rows // SB_HEADS, SEG_W))],
        scratch_shapes=[pltpu.VMEM((2, tq, LANES), BF16), pltpu.VMEM((2, tq, LANES), F32),
                        pltpu.VMEM((2, tq, 1), F32),
                        pltpu.VMEM((2, 2 * ppi, SEG_W, page), F32), pltpu.VMEM((2, 2 * ppi, SEG_W, page), F32),
                        pltpu.SemaphoreType.DMA((2,)), pltpu.SemaphoreType.DMA((2,)),
                        pltpu.VMEM((rows, SEG_W), BF16), pltpu.VMEM((rows, SEG_W), F32),
                        pltpu.VMEM((rows, 1), F32)],
    )
    return pl.pallas_call(
        functools.partial(_sb_fused_kernel, tq=tq, tk=tk, n_pages=n_pages, ppi=ppi, diag_pairs=diag_pairs),
        grid_spec=grid_spec,
        out_shape=[jax.ShapeDtypeStruct((batch * seq, SEG_W), F32),
                   jax.ShapeDtypeStruct((db, rows // SB_HEADS, SEG_W), F32)],
        compiler_params=pltpu.CompilerParams(dimension_semantics=("arbitrary", "arbitrary", "arbitrary"),
                                             vmem_limit_bytes=FUSED_VMEM_LIMIT),
        name="sb_fused",
    )(page_table, sb_bias, sqb, skb_t, svb, sg, q_rep, bias_col, cache_kt, cache_vt)


def _sb_sample_finish_kernel(bias_ref, q_ref, kn_ref, vn_ref, past_ref, sg_ref, o_ref, *, bb, page, dt):
    rows = dt * SB_HEADS
    own_rows = _head_rows_mask(rows)
    rq = lax.broadcasted_iota(jnp.int32, (rows, page), 0) // SB_HEADS
    cm = lax.broadcasted_iota(jnp.int32, (rows, page), 1)
    zero_carry = jnp.zeros((rows, 1), F32)
    for e in range(bb):
        q = q_ref[e]
        qbd = jnp.where(own_rows, q, jnp.zeros_like(q))
        zs = _span_logits(qbd, bias_ref[...], [kn_ref[e]])
        c0, acc = _span_accumulate(zs, _span_suffixes(zs, cm < rq), [vn_ref[e]], zero_carry,
                                   jnp.zeros((rows, SEG_W), F32), cm < rq)
        past = jnp.broadcast_to(past_ref[e][:, None, :], (dt, SB_HEADS, SEG_W)).reshape(rows, SEG_W)
        own = jnp.where(own_rows, acc + jnp.exp2(-c0) * past, 0.0)
        o = jnp.sum(own.reshape(dt, SB_HEADS, SEG_W), axis=1)
        o_ref[e] = o * _silu(sg_ref[e])


def _sb_sample_finish(bias_col, q_rep, k_new, v_new, past, sg):
    db, rows, _ = q_rep.shape
    page = k_new.shape[2]
    dt = sg.shape[1]
    bb = 8
    per_b = lambda blk: pl.BlockSpec((bb,) + blk, lambda i: (i, 0, 0))
    return pl.pallas_call(
        functools.partial(_sb_sample_finish_kernel, bb=bb, page=page, dt=dt),
        grid=(db // bb,),
        in_specs=[pl.BlockSpec((rows, 1), lambda i: (0, 0)),
                  per_b((rows, SEG_W)), per_b((SEG_W, page)), per_b((SEG_W, page)),
                  per_b((dt, SEG_W)), per_b((dt, SEG_W))],
        out_specs=per_b((dt, SEG_W)),
        out_shape=jax.ShapeDtypeStruct((db, dt, SEG_W), F32),
        compiler_params=pltpu.CompilerParams(dimension_semantics=("parallel",), vmem_limit_bytes=VMEM_LIMIT),
        name="sb_sample_finish",
    )(bias_col, q_rep, k_new, v_new, past, sg)


def _out_kernel(yr_ref, ys_ref, h_ref, p_ref, wo_ref, wp_ref, wg_ref, gp_ref, gf_ref, o_ref):
    y = _dot(yr_ref[...].astype(BF16), wo_ref[:SEG_W, :]) + _dot(ys_ref[...].astype(BF16), wo_ref[SEG_W:, :])
    h1 = h_ref[...] + y
    e = _dot(p_ref[...].astype(BF16), wp_ref[...])
    e = (e * lax.rsqrt(jnp.mean(e * e, axis=-1, keepdims=True) + EPS)) * gp_ref[...]
    gate = 1.0 / (1.0 + jnp.exp(-_dot(h1.astype(BF16), wg_ref[...])))
    h2 = h1 + e * gate
    o_ref[...] = (h2 * lax.rsqrt(jnp.mean(h2 * h2, axis=-1, keepdims=True) + EPS)) * gf_ref[...]


def _out_proj(yr, ys, h, p, wo_bf, wp_bf, wg_bf, g_ple, g_final, tm):
    n, d = h.shape
    d_ple = p.shape[1]
    row = lambda i: (i, 0)
    const = lambda i: (0, 0)
    return pl.pallas_call(
        _out_kernel,
        grid=(n // tm,),
        in_specs=[pl.BlockSpec((tm, SEG_W), row), pl.BlockSpec((tm, SEG_W), row), pl.BlockSpec((tm, d), row),
                  pl.BlockSpec((tm, d_ple), row), pl.BlockSpec((2 * SEG_W, d), const),
                  pl.BlockSpec((d_ple, d), const), pl.BlockSpec((d, d), const),
                  pl.BlockSpec((1, d), const), pl.BlockSpec((1, d), const)],
        out_specs=pl.BlockSpec((tm, d), row),
        out_shape=jax.ShapeDtypeStruct((n, d), F32),
        compiler_params=pltpu.CompilerParams(dimension_semantics=("parallel",), vmem_limit_bytes=VMEM_LIMIT),
        name="out_proj_ple_norm",
    )(yr, ys, h, p, wo_bf, wp_bf, wg_bf, g_ple, g_final)


def kernel(x_prompt, x_sample, cache_sb_k, cache_sb_v, state_ret, page_table, p_prompt, p_sample,
           g_norm, w_in, sb_bias, g_ret_gn, w_out, w_ple, g_ple, w_ple_gate, g_final):
    b, t, d = x_prompt.shape
    db, dt, _ = x_sample.shape
    depth = w_in.shape[0]
    assert depth == 1, "single-layer trunk"
    n_pool, page = cache_sb_k.shape[1], cache_sb_k.shape[2]
    past = page_table.shape[1] * page
    tm_p = min(256, t)
    tm_s = min(256, db * dt)

    half = RET_D // 2
    inv = ROPE_BASE ** (-jnp.arange(half, dtype=F32) / half)
    inv128 = jnp.concatenate([inv, inv])[None, :]
    cos_p, sin_p = _rope_table(inv128, t, 0, t)
    cos_s, sin_s = _rope_table(inv128, tm_s, past, dt)

    w_bf = w_in[0].astype(BF16)
    wkv_t = w_bf[:, 5 * SEG_W:6 * SEG_W].T
    gn = g_norm[0][None, :]
    ggn = g_ret_gn[0][None, :]
    wo_bf = w_out[0].astype(BF16)
    wp_bf = w_ple[0].astype(BF16)
    wg_bf = w_ple_gate[0].astype(BF16)
    gp = g_ple[0][None, :]
    gf = g_final[None, :]

    xp = x_prompt.reshape(b * t, d)
    rq, rk, rv, rg, sqb, sk_t, sv_t, sg, skb_t, svb = _project(xp, gn, w_bf, wkv_t, cos_p, sin_p, t // tm_p, tm_p, seq=t)
    xs = x_sample.reshape(db * dt, d)
    rq_s, rk_s, rv_s, rg_s, sqb_s, sk_s, sv_s, sg_s, skb_s, svb_s = _project(xs, gn, w_bf, wkv_t, cos_s, sin_s, 1, tm_s)

    q_rep = jnp.repeat(sqb_s.reshape(db, dt, SEG_W), SB_HEADS, axis=1)
    bias_col = jnp.tile(sb_bias[0] * LOG2E, dt)[:, None]
    cache_kt = jnp.transpose(cache_sb_k[0], (0, 2, 3, 1)).reshape(n_pool, SEG_W, page)
    cache_vt = jnp.transpose(cache_sb_v[0], (0, 2, 3, 1)).reshape(n_pool, SEG_W, page)
    tq = min(2048, t)
    ys, past_s = _sb_fused(page_table, sb_bias[0], sqb, skb_t.reshape(b * (SB_HEADS // 2), LANES, t), svb, sg,
                           q_rep, bias_col, cache_kt, cache_vt, b, t, tq, min(256, tq))
    pad = ((0, 0), (0, 0), (0, page - dt))
    k_new = jnp.pad(jnp.swapaxes(skb_s.reshape(db, dt, SEG_W), 1, 2), pad)
    v_new = jnp.pad(jnp.swapaxes(svb_s.reshape(db, dt, SEG_W), 1, 2), pad)
    ys_s = _sb_sample_finish(bias_col, q_rep, k_new, v_new, past_s, sg_s.reshape(db, dt, SEG_W))

    yr, s_fin = _retention_prompt(rq, rk, rv, rg, ggn, b, t)
    yr_s, s_new = _retention_sample(rq_s, rk_s, rv_s, rg_s, ggn, state_ret[0], dt)
    y_prompt = _out_proj(yr.reshape(b * t, SEG_W), ys, xp, p_prompt[0].reshape(b * t, -1),
                         wo_bf, wp_bf, wg_bf, gp, gf, min(512, b * t))
    y_sample = _out_proj(yr_s, ys_s.reshape(db * dt, SEG_W), xs, p_sample[0].reshape(db * dt, -1),
                         wo_bf, wp_bf, wg_bf, gp, gf, min(512, db * dt))

    return (y_prompt.reshape(b, t, d),
            y_sample.reshape(db, dt, d),
            jnp.transpose(sk_t.reshape(1, b, SB_HEADS, SB_DH, t), (0, 1, 4, 2, 3)),
            jnp.transpose(sv_t.reshape(1, b, SB_HEADS, SB_DH, t), (0, 1, 4, 2, 3)),
            sk_s.reshape(1, db, dt, SB_HEADS, SB_DH),
            sv_s.reshape(1, db, dt, SB_HEADS, SB_DH),
            s_fin.reshape(1, b, RET_HEADS, RET_D, RET_D),
            s_new[None])
```
